```python
import jax, jax.numpy as jnp
from jax import lax
import numpy as np

D_MODEL = 1024
BATCH = 4
SEQ = 4096
DEPTH = 1

N_HEADS = 8
QK_NOPE_DIM = 128
QK_ROPE_DIM = 64
V_HEAD_DIM = 128
Q_LORA_RANK = 3 * D_MODEL // 8
KV_LORA_RANK = D_MODEL // 4
ROPE_THETA = 10000.0
Q_BLOCK = 128
CONV_CHANNELS = D_MODEL
CONV_WIDTH = 31
D_FF = 4 * D_MODEL
N_BRANCHES = 2
EPS = 1e-6
IN_SIZES = (Q_LORA_RANK, KV_LORA_RANK, QK_ROPE_DIM, 2 * CONV_CHANNELS, N_BRANCHES * D_MODEL)
IN_OFFSETS = tuple(int(o) for o in np.cumsum(IN_SIZES)[:-1])
D_IN = int(sum(IN_SIZES))

kernel_name = "hybrid_mla_conformer_conv_gated_block"


def rms_norm(x, g):
    x32 = x.astype(jnp.float32)
    y = x32 * lax.rsqrt(jnp.mean(x32 * x32, axis=-1, keepdims=True) + EPS)
    return y.astype(x.dtype) * g


def layer_norm(x, g, b):
    x32 = x.astype(jnp.float32)
    mu = jnp.mean(x32, axis=-1, keepdims=True)
    xc = x32 - mu
    var = jnp.mean(xc * xc, axis=-1, keepdims=True)
    y = xc * lax.rsqrt(var + EPS)
    return y.astype(x.dtype) * g + b


def rope_tables(positions):
    inv_freq = ROPE_THETA ** (-jnp.arange(0, QK_ROPE_DIM, 2, dtype=jnp.float32) / QK_ROPE_DIM)
    ang = positions.astype(jnp.float32)[..., None] * inv_freq
    return jnp.cos(ang), jnp.sin(ang)


def apply_rope(x, cos, sin):
    half = x.shape[-1] // 2
    x1, x2 = x[..., :half], x[..., half:]
    cos = cos.astype(x.dtype)
    sin = sin.astype(x.dtype)
    return jnp.concatenate([x1 * cos - x2 * sin, x2 * cos + x1 * sin], axis=-1)


def causal_block_attention(q_nope, q_rope, k_nope, k_rope, v):
    B, H, S, Dn = q_nope.shape
    Dr = q_rope.shape[-1]
    Dv = v.shape[-1]
    n_blk = S // Q_BLOCK
    qn = q_nope.reshape(B, H, n_blk, Q_BLOCK, Dn).transpose(2, 0, 1, 3, 4)
    qr = q_rope.reshape(B, H, n_blk, Q_BLOCK, Dr).transpose(2, 0, 1, 3, 4)
    scale = (Dn + Dr) ** -0.5
    key_idx = jnp.arange(S)

    def one_block(args):
        qn_b, qr_b, blk = args
        s = jnp.einsum('bhqd,bhkd->bhqk', qn_b, k_nope) + jnp.einsum('bhqd,bkd->bhqk', qr_b, k_rope)
        s = s.astype(jnp.float32) * scale
        q_idx = blk * Q_BLOCK + jnp.arange(Q_BLOCK)
        mask = key_idx[None, :] <= q_idx[:, None]
        s = jnp.where(mask, s, -jnp.inf)
        p = jax.nn.softmax(s, axis=-1).astype(v.dtype)
        return jnp.einsum('bhqk,bhkd->bhqd', p, v)

    out = lax.map(one_block, (qn, qr, jnp.arange(n_blk)))
    return out.transpose(1, 0, 3, 2, 4).reshape(B, S, H * Dv)


def hybrid_mixer(h, cos, sin, w_in, q_norm, w_uq, kv_norm, w_uk, w_uv, w_o_attn,
                 conv_w, conv_b, conv_ln_g, conv_ln_b, w_pw2, b_pw2, w_out):
    B, S, _ = h.shape
    z = h @ w_in
    c_q, c_kv, k_r, u_glu, gate_logits = jnp.split(z, IN_OFFSETS, axis=-1)

    q = (rms_norm(c_q, q_norm) @ w_uq).reshape(B, S, N_HEADS, QK_NOPE_DIM + QK_ROPE_DIM)
    q_nope = q[..., :QK_NOPE_DIM]
    q_rope = apply_rope(q[..., QK_NOPE_DIM:], cos[:, :, None, :], sin[:, :, None, :])
    c_kv_n = rms_norm(c_kv, kv_norm)
    k_nope = jnp.einsum('bsc,chd->bhsd', c_kv_n, w_uk)
    v = jnp.einsum('bsc,chd->bhsd', c_kv_n, w_uv)
    k_rope = apply_rope(k_r, cos, sin)
    attn = causal_block_attention(q_nope.transpose(0, 2, 1, 3), q_rope.transpose(0, 2, 1, 3),
                                  k_nope, k_rope, v)
    y_attn = attn @ w_o_attn

    a, b = jnp.split(u_glu, 2, axis=-1)
    u = a * jax.nn.sigmoid(b)
    u = lax.conv_general_dilated(u, conv_w, window_strides=(1,),
                                 padding=[(CONV_WIDTH - 1, 0)],
                                 dimension_numbers=('NWC', 'WIO', 'NWC'),
                                 feature_group_count=CONV_CHANNELS) + conv_b
    u = jax.nn.silu(layer_norm(u, conv_ln_g, conv_ln_b))
    y_conv = u @ w_pw2 + b_pw2

    g_attn, g_conv = jnp.split(gate_logits, N_BRANCHES, axis=-1)
    merged = jax.nn.sigmoid(g_attn) * y_attn + jax.nn.sigmoid(g_conv) * y_conv
    return merged @ w_out


def squared_relu_mlp(h, w_ff1, w_ff2):
    return jnp.square(jax.nn.relu(h @ w_ff1)) @ w_ff2


def setup_inputs(seed: int = 0) -> dict:
    key = jax.random.key(seed)
    ks = jax.random.split(key, 24)
    f32 = jnp.float32

    def nrm(k, shape, scale):
        return jax.random.normal(k, shape, f32) * scale

    def gain(k, shape):
        return 1.0 + 0.01 * jax.random.normal(k, shape, f32)

    L = DEPTH
    x = jax.random.normal(ks[0], (BATCH, SEQ, D_MODEL), f32)
    offset = jax.random.randint(ks[1], (BATCH, 1), 0, 1024, dtype=jnp.int32)
    positions = (offset + jnp.arange(SEQ, dtype=jnp.int32)[None, :]).astype(jnp.int32)
    return {
        "x": x,
        "positions": positions,
        "norm_mix_pre": gain(ks[2], (L, D_MODEL)),
        "w_in": nrm(ks[3], (L, D_MODEL, D_IN), D_MODEL ** -0.5),
        "q_norm": gain(ks[4], (L, Q_LORA_RANK)),
        "w_uq": nrm(ks[5], (L, Q_LORA_RANK, N_HEADS * (QK_NOPE_DIM + QK_ROPE_DIM)), Q_LORA_RANK ** -0.5),
        "kv_norm": gain(ks[6], (L, KV_LORA_RANK)),
        "w_uk": nrm(ks[7], (L, KV_LORA_RANK, N_HEADS, QK_NOPE_DIM), KV_LORA_RANK ** -0.5),
        "w_uv": nrm(ks[8], (L, KV_LORA_RANK, N_HEADS, V_HEAD_DIM), KV_LORA_RANK ** -0.5),
        "w_o_attn": nrm(ks[9], (L, N_HEADS * V_HEAD_DIM, D_MODEL), (N_HEADS * V_HEAD_DIM) ** -0.5),
        "conv_w": nrm(ks[10], (L, CONV_WIDTH, 1, CONV_CHANNELS), CONV_WIDTH ** -0.5),
        "conv_b": nrm(ks[11], (L, CONV_CHANNELS), 0.01),
        "conv_ln_g": gain(ks[12], (L, CONV_CHANNELS)),
        "conv_ln_b": nrm(ks[13], (L, CONV_CHANNELS), 0.01),
        "w_pw2": nrm(ks[14], (L, CONV_CHANNELS, D_MODEL), CONV_CHANNELS ** -0.5),
        "b_pw2": nrm(ks[15], (L, D_MODEL), 0.01),
        "w_out": nrm(ks[16], (L, D_MODEL, D_MODEL), D_MODEL ** -0.5),
        "norm_mix_post": gain(ks[17], (L, D_MODEL)),
        "norm_mlp_pre": gain(ks[18], (L, D_MODEL)),
        "w_ff1": nrm(ks[19], (L, D_MODEL, D_FF), D_MODEL ** -0.5),
        "w_ff2": nrm(ks[20], (L, D_FF, D_MODEL), D_FF ** -0.5),
        "norm_mlp_post": gain(ks[21], (L, D_MODEL)),
    }


def reference(x, positions, norm_mix_pre, w_in, q_norm, w_uq, kv_norm, w_uk, w_uv, w_o_attn,
              conv_w, conv_b, conv_ln_g, conv_ln_b, w_pw2, b_pw2, w_out, norm_mix_post,
              norm_mlp_pre, w_ff1, w_ff2, norm_mlp_post):
    cos, sin = rope_tables(positions)
    for l in range(DEPTH):
        h = rms_norm(x, norm_mix_pre[l])
        m = hybrid_mixer(h, cos, sin, w_in[l], q_norm[l], w_uq[l], kv_norm[l], w_uk[l], w_uv[l],
                         w_o_attn[l], conv_w[l], conv_b[l], conv_ln_g[l], conv_ln_b[l],
                         w_pw2[l], b_pw2[l], w_out[l])
        x = x + rms_norm(m, norm_mix_post[l])
        f = squared_relu_mlp(rms_norm(x, norm_mlp_pre[l]), w_ff1[l], w_ff2[l])
        x = x + rms_norm(f, norm_mlp_post[l])
    return x
```

```python
import functools
import math

import jax
import jax.numpy as jnp
import numpy as np
from jax import lax
from jax.experimental import pallas as pl
from jax.experimental.pallas import tpu as pltpu

D_MODEL = 1024
N_HEADS = 8
QK_NOPE_DIM = 128
QK_ROPE_DIM = 64
V_HEAD_DIM = 128
Q_LORA_RANK = 384
KV_LORA_RANK = 256
ROPE_THETA = 10000.0
CONV_CHANNELS = 1024
CONV_WIDTH = 31
D_FF = 4096
EPS = 1e-6

HALF_ROPE = QK_ROPE_DIM // 2
ROPE_LANES = 4 * HALF_ROPE
HEAD_SLOT = QK_NOPE_DIM + ROPE_LANES
SMALL_COLS = Q_LORA_RANK + KV_LORA_RANK + ROPE_LANES
HALO = 32
SUBLANES = 8
LANES = 128
CONV_ROWS = 64
LN_ROWS = 16

Q_SCALE = (QK_NOPE_DIM + QK_ROPE_DIM) ** -0.5 * math.log2(math.e)

TM_IN = 512
TM_POST = 512
TQ = 512
TK = 512
FF_CHUNK = 1024

VMEM_LIMIT = 60 * 1024 * 1024

_bf16 = jnp.bfloat16
_f32 = jnp.float32


def _dot(a, b):
    return jnp.dot(a, b, preferred_element_type=_f32)


def _rms(x, g):
    return x * lax.rsqrt(jnp.mean(x * x, axis=-1, keepdims=True) + EPS) * g


def _inproj_kernel(x_ref, pos_ref, invf_ref, g_pre_ref, w_in_ref, qn_g_ref, w_uq_ref,
                   kvn_g_ref, w_uk_ref, w_uv_ref, cw_ref, cb_ref, lng_ref, lnb_ref,
                   w_pw2_ref, b_pw2_ref,
                   q_ref, k_ref, v_ref, sg_ref, gc_ref,
                   ubuf, cacc, cbuf, *, tiles_per_seq):
    tm = x_ref.shape[0]
    h = _rms(x_ref[...], g_pre_ref[...]).astype(_bf16)

    ang = pos_ref[...].astype(_f32) * invf_ref[...]
    lane = lax.broadcasted_iota(jnp.int32, ang.shape, 1)
    sin = jnp.sin(ang)
    rope4 = jnp.where(lane < 2 * HALF_ROPE, jnp.cos(ang),
                      jnp.where(lane < 3 * HALF_ROPE, -sin, sin))

    zs = _dot(h, w_in_ref[:, 0:SMALL_COLS])
    cq = zs[:, 0:Q_LORA_RANK]
    ckv = zs[:, Q_LORA_RANK:Q_LORA_RANK + KV_LORA_RANK]
    kr4 = zs[:, Q_LORA_RANK + KV_LORA_RANK:SMALL_COLS]

    q = _dot(_rms(cq, qn_g_ref[...]).astype(_bf16), w_uq_ref[...])
    rope4_q = rope4 * Q_SCALE
    for hd in range(N_HEADS):
        lo = hd * HEAD_SLOT
        q_ref[:, lo:lo + QK_NOPE_DIM] = (q[:, lo:lo + QK_NOPE_DIM] * Q_SCALE).astype(_bf16)
        q_ref[:, lo + QK_NOPE_DIM:lo + HEAD_SLOT] = (
            q[:, lo + QK_NOPE_DIM:lo + HEAD_SLOT] * rope4_q).astype(_bf16)

    kp = kr4 * rope4
    kdup = (kp + pltpu.roll(kp, 2 * HALF_ROPE, 1)).astype(_bf16)
    ckvn = _rms(ckv, kvn_g_ref[...]).astype(_bf16)
    kn = _dot(ckvn, w_uk_ref[...])
    for hd in range(N_HEADS):
        lo = hd * HEAD_SLOT
        k_ref[:, lo:lo + QK_NOPE_DIM] = kn[:, hd * QK_NOPE_DIM:(hd + 1) * QK_NOPE_DIM].astype(_bf16)
        k_ref[:, lo + QK_NOPE_DIM:lo + HEAD_SLOT] = kdup
    v_ref[...] = _dot(ckvn, w_uv_ref[...]).astype(_bf16)

    c0 = SMALL_COLS
    C = CONV_CHANNELS
    sg_ref[...] = jax.nn.sigmoid(_dot(h, w_in_ref[:, c0 + 2 * C:c0 + 3 * C])).astype(_bf16)

    @pl.when(pl.program_id(0) % tiles_per_seq == 0)
    def _():
        ubuf[0:HALO, :] = jnp.zeros((HALO, C), _f32)

    a = _dot(h, w_in_ref[:, c0:c0 + C])
    b = _dot(h, w_in_ref[:, c0 + C:c0 + 2 * C])
    ubuf[HALO:HALO + tm, :] = a * jax.nn.sigmoid(b)

    shift = HALO - (CONV_WIDTH - 1)
    win = CONV_ROWS + HALO

    def conv_rows(r, carry):
        r0 = pl.multiple_of(r * CONV_ROWS, CONV_ROWS)

        def conv_lanes(c, carry):
            l0 = pl.multiple_of(c * LANES, LANES)
            window = ubuf[pl.ds(r0, win), pl.ds(l0, LANES)]
            acc = jnp.broadcast_to(cb_ref[:, pl.ds(l0, LANES)], (CONV_ROWS, LANES))
            for phase in range(SUBLANES):
                rolled = window if phase == 0 else pltpu.roll(window, win - phase, 0)
                for a in range(win // SUBLANES):
                    t = SUBLANES * a + phase - shift
                    if 0 <= t < CONV_WIDTH:
                        acc = acc + (rolled[SUBLANES * a:SUBLANES * a + CONV_ROWS, :]
                                     * cw_ref[pl.ds(t, 1), pl.ds(l0, LANES)])
            cacc[pl.ds(r0, CONV_ROWS), pl.ds(l0, LANES)] = acc
            return carry

        return lax.fori_loop(0, C // LANES, conv_lanes, carry)

    lax.fori_loop(0, tm // CONV_ROWS, conv_rows, 0)
    ubuf[0:HALO, :] = ubuf[tm:tm + HALO, :]

    def ln_rows(r, carry):
        r0 = pl.multiple_of(r * LN_ROWS, LN_ROWS)
        acc = cacc[pl.ds(r0, LN_ROWS), :]
        mu = jnp.mean(acc, axis=-1, keepdims=True)
        xc = acc - mu
        var = jnp.mean(xc * xc, axis=-1, keepdims=True)
        y = xc * lax.rsqrt(var + EPS) * lng_ref[...] + lnb_ref[...]
        cbuf[pl.ds(r0, LN_ROWS), :] = (y * jax.nn.sigmoid(y)).astype(_bf16)
        return carry

    lax.fori_loop(0, tm // LN_ROWS, ln_rows, 0)

    y_conv = _dot(cbuf[...], w_pw2_ref[...]) + b_pw2_ref[...]
    g_conv = jax.nn.sigmoid(_dot(h, w_in_ref[:, c0 + 3 * C:c0 + 4 * C]))
    gc_ref[...] = (g_conv * y_conv).astype(_bf16)


def _attn_kernel(q_ref, k_ref, v_ref, o_ref):
    seq = q_ref.shape[0]
    row = lax.broadcasted_iota(jnp.int32, (TQ, TK), 0)
    col = lax.broadcasted_iota(jnp.int32, (TQ, TK), 1)
    diag_mask = col <= row

    def q_tile(i, carry):
        q0 = pl.multiple_of(i * TQ, TQ)
        q = q_ref[pl.ds(q0, TQ), :]

        def scores(j):
            k0 = pl.multiple_of(j * TK, TK)
            s = lax.dot_general(q, k_ref[pl.ds(k0, TK), :], (((1,), (1,)), ((), ())),
                                preferred_element_type=_f32)
            return s, k0

        def update(s, k0, m, l, acc):
            m_new = jnp.maximum(m, jnp.max(s, axis=-1, keepdims=True))
            alpha = jnp.exp2(m - m_new)
            p = jnp.exp2(s - m_new)
            l = alpha * l + jnp.sum(p, axis=-1, keepdims=True)
            acc = alpha * acc + _dot(p.astype(_bf16), v_ref[pl.ds(k0, TK), :])
            return m_new, l, acc

        def kv_step(j, mla):
            s, k0 = scores(j)
            return update(s, k0, *mla)

        init = (jnp.full((TQ, 1), -jnp.inf, _f32), jnp.zeros((TQ, 1), _f32),
                jnp.zeros((TQ, V_HEAD_DIM), _f32))
        mla = lax.fori_loop(0, i, kv_step, init)
        s, k0 = scores(i)
        m, l, acc = update(jnp.where(diag_mask, s, -jnp.inf), k0, *mla)
        o_ref[pl.ds(q0, TQ), :] = (acc / l).astype(_bf16)
        return carry

    lax.fori_loop(0, seq // TQ, q_tile, 0)


def _post_kernel(attn_ref, sg_ref, gc_ref, x_ref, w_o_ref, w_out_ref, g_mix_post_ref,
                 g_mlp_pre_ref, w_ff1_ref, w_ff2_ref, g_mlp_post_ref, out_ref):
    y_attn = _dot(attn_ref[...], w_o_ref[...])
    merged = sg_ref[...].astype(_f32) * y_attn + gc_ref[...].astype(_f32)
    m = _dot(merged.astype(_bf16), w_out_ref[...])
    x1 = x_ref[...] + _rms(m, g_mix_post_ref[...])
    h2 = _rms(x1, g_mlp_pre_ref[...]).astype(_bf16)
    f = jnp.zeros(x1.shape, _f32)
    for c in range(D_FF // FF_CHUNK):
        hc = jnp.maximum(_dot(h2, w_ff1_ref[:, c * FF_CHUNK:(c + 1) * FF_CHUNK]), 0.0)
        f = f + _dot((hc * hc).astype(_bf16), w_ff2_ref[c * FF_CHUNK:(c + 1) * FF_CHUNK, :])
    out_ref[...] = x1 + _rms(f, g_mlp_post_ref[...])


def _resident(shape):
    return pl.BlockSpec(shape, lambda *_: (0,) * len(shape), pipeline_mode=pl.Buffered(1))


def _rows(tm, cols):
    return pl.BlockSpec((tm, cols), lambda i: (i, 0))


def kernel(x, positions, norm_mix_pre, w_in, q_norm, w_uq, kv_norm, w_uk, w_uv, w_o_attn,
           conv_w, conv_b, conv_ln_g, conv_ln_b, w_pw2, b_pw2, w_out, norm_mix_post,
           norm_mlp_pre, w_ff1, w_ff2, norm_mlp_post):
    B, S, D = x.shape
    T = B * S
    assert D == D_MODEL and w_in.shape[0] == 1
    assert S % TM_IN == 0 and S % TQ == 0 and T % TM_POST == 0 and TQ == TK
    C = CONV_CHANNELS
    H = N_HEADS

    w_in0 = w_in[0]
    o_q, o_kv, o_kr = Q_LORA_RANK, Q_LORA_RANK + KV_LORA_RANK, Q_LORA_RANK + KV_LORA_RANK + QK_ROPE_DIM
    k1 = w_in0[:, o_kv:o_kv + HALF_ROPE]
    k2 = w_in0[:, o_kv + HALF_ROPE:o_kr]
    w_in_k = jnp.concatenate([w_in0[:, :o_kv], k1, k2, k2, k1, w_in0[:, o_kr:]], axis=1).astype(_bf16)

    wq = w_uq[0].reshape(Q_LORA_RANK, H, QK_NOPE_DIM + QK_ROPE_DIM)
    r1 = wq[:, :, QK_NOPE_DIM:QK_NOPE_DIM + HALF_ROPE]
    r2 = wq[:, :, QK_NOPE_DIM + HALF_ROPE:]
    w_uq_k = jnp.concatenate([wq[:, :, :QK_NOPE_DIM], r1, r2, r2, r1], axis=2)
    w_uq_k = w_uq_k.reshape(Q_LORA_RANK, H * HEAD_SLOT).astype(_bf16)
    w_uk_k = w_uk[0].reshape(KV_LORA_RANK, H * QK_NOPE_DIM).astype(_bf16)
    w_uv_k = w_uv[0].reshape(KV_LORA_RANK, H * V_HEAD_DIM).astype(_bf16)
    cw = jnp.pad(conv_w[0].reshape(CONV_WIDTH, C), ((0, 1), (0, 0)))

    inv_freq = ROPE_THETA ** (-jnp.arange(0, QK_ROPE_DIM, 2, dtype=_f32) / QK_ROPE_DIM)
    invf4 = jnp.tile(inv_freq, 4).reshape(1, ROPE_LANES)

    row = lambda a: a.reshape(1, -1)
    x2 = x.reshape(T, D)
    pos2 = positions.reshape(T, 1)

    q, k, v, sg, gc = pl.pallas_call(
        functools.partial(_inproj_kernel, tiles_per_seq=S // TM_IN),
        grid=(T // TM_IN,),
        in_specs=[
            _rows(TM_IN, D), _rows(TM_IN, 1), _resident((1, ROPE_LANES)), _resident((1, D)),
            _resident(w_in_k.shape), _resident((1, Q_LORA_RANK)), _resident(w_uq_k.shape),
            _resident((1, KV_LORA_RANK)), _resident(w_uk_k.shape), _resident(w_uv_k.shape),
            _resident(cw.shape), _resident((1, C)), _resident((1, C)), _resident((1, C)),
            _resident((C, D)), _resident((1, D)),
        ],
        out_specs=[_rows(TM_IN, H * HEAD_SLOT), _rows(TM_IN, H * HEAD_SLOT),
                   _rows(TM_IN, H * V_HEAD_DIM), _rows(TM_IN, D), _rows(TM_IN, D)],
        out_shape=[jax.ShapeDtypeStruct((T, H * HEAD_SLOT), _bf16),
                   jax.ShapeDtypeStruct((T, H * HEAD_SLOT), _bf16),
                   jax.ShapeDtypeStruct((T, H * V_HEAD_DIM), _bf16),
                   jax.ShapeDtypeStruct((T, D), _bf16),
                   jax.ShapeDtypeStruct((T, D), _bf16)],
        scratch_shapes=[pltpu.VMEM((HALO + TM_IN, C), _f32), pltpu.VMEM((TM_IN, C), _f32),
                        pltpu.VMEM((TM_IN, C), _bf16)],
        compiler_params=pltpu.CompilerParams(dimension_semantics=("arbitrary",),
                                             vmem_limit_bytes=VMEM_LIMIT),
        name="inproj",
    )(x2, pos2, invf4, row(norm_mix_pre[0]), w_in_k, row(q_norm[0]), w_uq_k, row(kv_norm[0]),
      w_uk_k, w_uv_k, cw, row(conv_b[0]), row(conv_ln_g[0]), row(conv_ln_b[0]),
      w_pw2[0].astype(_bf16), row(b_pw2[0]))

    attn = pl.pallas_call(
        _attn_kernel,
        grid=(B, H),
        in_specs=[pl.BlockSpec((S, HEAD_SLOT), lambda b, h: (b, h)),
                  pl.BlockSpec((S, HEAD_SLOT), lambda b, h: (b, h)),
                  pl.BlockSpec((S, V_HEAD_DIM), lambda b, h: (b, h))],
        out_specs=pl.BlockSpec((S, V_HEAD_DIM), lambda b, h: (b, h)),
        out_shape=jax.ShapeDtypeStruct((T, H * V_HEAD_DIM), _bf16),
        compiler_params=pltpu.CompilerParams(dimension_semantics=("arbitrary", "arbitrary"),
                                             vmem_limit_bytes=VMEM_LIMIT),
        name="attn",
    )(q, k, v)

    out = pl.pallas_call(
        _post_kernel,
        grid=(T // TM_POST,),
        in_specs=[_rows(TM_POST, H * V_HEAD_DIM), _rows(TM_POST, D), _rows(TM_POST, D),
                  _rows(TM_POST, D), _resident((H * V_HEAD_DIM, D)), _resident((D, D)),
                  _resident((1, D)), _resident((1, D)), _resident((D, D_FF)),
                  _resident((D_FF, D)), _resident((1, D))],
        out_specs=_rows(TM_POST, D),
        out_shape=jax.ShapeDtypeStruct((T, D), _f32),
        compiler_params=pltpu.CompilerParams(dimension_semantics=("arbitrary",),
                                             vmem_limit_bytes=VMEM_LIMIT),
        name="post",
    )(attn, sg, gc, x2, w_o_attn[0].astype(_bf16), w_out[0].astype(_bf16),
      row(norm_mix_post[0]), row(norm_mlp_pre[0]), w_ff1[0].astype(_bf16),
      w_ff2[0].astype(_bf16), row(norm_mlp_post[0]))
    return out.reshape(B, S, D)
```

```python
import functools
import math

import jax
import jax.numpy as jnp
import numpy as np
from jax import lax
from jax.experimental import pallas as pl
from jax.experimental.pallas import tpu as pltpu

D_MODEL = 1024
N_HEADS = 8
QK_NOPE_DIM = 128
QK_ROPE_DIM = 64
V_HEAD_DIM = 128
Q_LORA_RANK = 384
KV_LORA_RANK = 256
ROPE_THETA = 10000.0
CONV_CHANNELS = 1024
CONV_WIDTH = 31
D_FF = 4096
EPS = 1e-6

HALF_ROPE = QK_ROPE_DIM // 2
ROPE_LANES = 4 * HALF_ROPE
HEAD_SLOT = QK_NOPE_DIM + ROPE_LANES
SMALL_COLS = Q_LORA_RANK + KV_LORA_RANK + ROPE_LANES
HALO = 32
SUBLANES = 8
LANES = 128
CONV_ROWS = 64
LN_ROWS = 16

Q_SCALE = (QK_NOPE_DIM + QK_ROPE_DIM) ** -0.5 * math.log2(math.e)
MASK_VALUE = -1e30

TM_IN = 512
TM_POST = 512
TQ = 256
TK = 512
ATT_HEADS = 2
ATT_UNROLL = 2
FF_CHUNK = 1024

VMEM_LIMIT = 60 * 1024 * 1024

_bf16 = jnp.bfloat16
_f32 = jnp.float32
_NT = (((1,), (1,)), ((), ()))


def _dot(a, b):
    return jnp.dot(a, b, preferred_element_type=_f32)


def _rms(x, g):
    return x * lax.rsqrt(jnp.mean(x * x, axis=-1, keepdims=True) + EPS) * g


def _inproj_kernel(x_ref, pos_ref, invf_ref, g_pre_ref, w_in_ref, qn_g_ref, w_uq_ref,
                   kvn_g_ref, w_uk_ref, w_uvt_ref, cw_ref, cb_ref, lng_ref, lnb_ref,
                   w_pw2_ref, b_pw2_ref,
                   q_ref, k_ref, vt_ref, sg_ref, gc_ref,
                   ubuf, cacc, cbuf, *, tiles_per_seq):
    tm = x_ref.shape[0]
    h = _rms(x_ref[...], g_pre_ref[...]).astype(_bf16)

    ang = pos_ref[...].astype(_f32) * invf_ref[...]
    lane = lax.broadcasted_iota(jnp.int32, ang.shape, 1)
    sin = jnp.sin(ang)
    rope4 = jnp.where(lane < 2 * HALF_ROPE, jnp.cos(ang),
                      jnp.where(lane < 3 * HALF_ROPE, -sin, sin))

    zs = _dot(h, w_in_ref[:, 0:SMALL_COLS])
    cq = zs[:, 0:Q_LORA_RANK]
    ckv = zs[:, Q_LORA_RANK:Q_LORA_RANK + KV_LORA_RANK]
    kr4 = zs[:, Q_LORA_RANK + KV_LORA_RANK:SMALL_COLS]

    q = _dot(_rms(cq, qn_g_ref[...]).astype(_bf16), w_uq_ref[...])
    rope4_q = rope4 * Q_SCALE
    for hd in range(N_HEADS):
        lo = hd * HEAD_SLOT
        q_ref[:, lo:lo + QK_NOPE_DIM] = (q[:, lo:lo + QK_NOPE_DIM] * Q_SCALE).astype(_bf16)
        q_ref[:, lo + QK_NOPE_DIM:lo + HEAD_SLOT] = (
            q[:, lo + QK_NOPE_DIM:lo + HEAD_SLOT] * rope4_q).astype(_bf16)

    kp = kr4 * rope4
    kdup = (kp + pltpu.roll(kp, 2 * HALF_ROPE, 1)).astype(_bf16)
    ckvn = _rms(ckv, kvn_g_ref[...]).astype(_bf16)
    kn = _dot(ckvn, w_uk_ref[...])
    for hd in range(N_HEADS):
        lo = hd * HEAD_SLOT
        k_ref[:, lo:lo + QK_NOPE_DIM] = kn[:, hd * QK_NOPE_DIM:(hd + 1) * QK_NOPE_DIM].astype(_bf16)
        k_ref[:, lo + QK_NOPE_DIM:lo + HEAD_SLOT] = kdup
    vt_ref[...] = lax.dot_general(w_uvt_ref[...], ckvn, _NT,
                                  preferred_element_type=_f32).astype(_bf16)

    c0 = SMALL_COLS
    C = CONV_CHANNELS
    sg_ref[...] = jax.nn.sigmoid(_dot(h, w_in_ref[:, c0 + 2 * C:c0 + 3 * C])).astype(_bf16)

    @pl.when(pl.program_id(0) % tiles_per_seq == 0)
    def _():
        ubuf[0:HALO, :] = jnp.zeros((HALO, C), _f32)

    a = _dot(h, w_in_ref[:, c0:c0 + C])
    b = _dot(h, w_in_ref[:, c0 + C:c0 + 2 * C])
    ubuf[HALO:HALO + tm, :] = a * jax.nn.sigmoid(b)

    shift = HALO - (CONV_WIDTH - 1)
    win = CONV_ROWS + HALO

    def conv_rows(r, carry):
        r0 = pl.multiple_of(r * CONV_ROWS, CONV_ROWS)

        def conv_lanes(c, carry):
            l0 = pl.multiple_of(c * LANES, LANES)
            window = ubuf[pl.ds(r0, win), pl.ds(l0, LANES)]
            acc = jnp.broadcast_to(cb_ref[:, pl.ds(l0, LANES)], (CONV_ROWS, LANES))
            for phase in range(SUBLANES):
                rolled = window if phase == 0 else pltpu.roll(window, win - phase, 0)
                for a in range(win // SUBLANES):
                    t = SUBLANES * a + phase - shift
                    if 0 <= t < CONV_WIDTH:
                        acc = acc + (rolled[SUBLANES * a:SUBLANES * a + CONV_ROWS, :]
                                     * cw_ref[pl.ds(t, 1), pl.ds(l0, LANES)])
            cacc[pl.ds(r0, CONV_ROWS), pl.ds(l0, LANES)] = acc
            return carry

        return lax.fori_loop(0, C // LANES, conv_lanes, carry)

    lax.fori_loop(0, tm // CONV_ROWS, conv_rows, 0)
    ubuf[0:HALO, :] = ubuf[tm:tm + HALO, :]

    def ln_rows(r, carry):
        r0 = pl.multiple_of(r * LN_ROWS, LN_ROWS)
        acc = cacc[pl.ds(r0, LN_ROWS), :]
        mu = jnp.mean(acc, axis=-1, keepdims=True)
        xc = acc - mu
        var = jnp.mean(xc * xc, axis=-1, keepdims=True)
        y = xc * lax.rsqrt(var + EPS) * lng_ref[...] + lnb_ref[...]
        cbuf[pl.ds(r0, LN_ROWS), :] = (y * jax.nn.sigmoid(y)).astype(_bf16)
        return carry

    lax.fori_loop(0, tm // LN_ROWS, ln_rows, 0)

    y_conv = _dot(cbuf[...], w_pw2_ref[...]) + b_pw2_ref[...]
    g_conv = jax.nn.sigmoid(_dot(h, w_in_ref[:, c0 + 3 * C:c0 + 4 * C]))
    gc_ref[...] = (g_conv * y_conv).astype(_bf16)


def _attn_stream(seq):
    qi, kj, first, last, kind = [], [], [], [], []
    for i in range(seq // TQ):
        n_blocks = (i * TQ + TQ - 1) // TK + 1
        for j in range(n_blocks):
            qi.append(i)
            kj.append(j)
            first.append(int(j == 0))
            last.append(int(j == n_blocks - 1))
            kind.append(0 if j < n_blocks - 1 else 1 + (i * TQ - j * TK) // TQ)
    pad = [0] * (2 * ATT_UNROLL)
    tables = [np.asarray(pad + t + pad, np.int32) for t in (qi, kj, first, last, kind)]
    return len(qi), tables


def _attn_mask_tiles():
    kk = np.arange(TK)[:, None]
    qq = np.arange(TQ)[None, :]
    tiles = [np.zeros((TK, TQ), np.float32)]
    for d in range(TK // TQ):
        tiles.append(np.where(kk <= qq + d * TQ, 0.0, MASK_VALUE).astype(np.float32))
    return np.stack(tiles)


def _attn_kernel(qi_ref, kj_ref, first_ref, last_ref, kind_ref,
                 q_ref, k_ref, vt_ref, mask_ref, o_ref, s_buf, p_buf, *, n_iter):
    U = ATT_UNROLL
    s_buf[...] = jnp.zeros(s_buf.shape, _f32)
    p_buf[...] = jnp.zeros(p_buf.shape, _bf16)

    def head_cols(ref, hd, width, rows):
        return ref[rows, hd * width:(hd + 1) * width]

    def stage_a(hd, e, slot):
        q0 = pl.multiple_of(qi_ref[e] * TQ, TQ)
        k0 = pl.multiple_of(kj_ref[e] * TK, TK)
        s = lax.dot_general(head_cols(k_ref, hd, HEAD_SLOT, pl.ds(k0, TK)),
                            head_cols(q_ref, hd, HEAD_SLOT, pl.ds(q0, TQ)), _NT,
                            preferred_element_type=_f32)
        s = s + mask_ref[kind_ref[e]]
        s_buf[hd, slot] = s
        return jnp.max(s, axis=0, keepdims=True)

    def stage_b(hd, e, slot, m, l, cmax):
        m_prev = jnp.where(first_ref[e] != 0, -jnp.inf, m)
        m_new = jnp.maximum(m_prev, cmax)
        alpha = jnp.exp2(m_prev - m_new)
        p = jnp.exp2(s_buf[hd, slot] - m_new)
        l = alpha * l + jnp.sum(p, axis=0, keepdims=True)
        p_buf[hd, slot] = p.astype(_bf16)
        return m_new, l, alpha

    def stage_c(hd, e, slot, acc, alpha):
        k0 = pl.multiple_of(kj_ref[e] * TK, TK)
        vt = vt_ref[hd * V_HEAD_DIM:(hd + 1) * V_HEAD_DIM, pl.ds(k0, TK)]
        return alpha * acc + _dot(vt, p_buf[hd, slot])

    def iteration(it, parity, carry):
        state = [dict(c) for c in carry]
        cmax_new = [[None] * U for _ in range(ATT_HEADS)]
        for u in range(U):
            for hd in range(ATT_HEADS):
                cmax_new[hd][u] = stage_a(hd, (it + 2) * U + u, parity * U + u)
        done = []
        for u in range(U):
            for hd in range(ATT_HEADS):
                st = state[hd]
                e = it * U + u
                st["acc"] = stage_c(hd, e, parity * U + u, st["acc"], st["alpha"][u])
                done.append((hd, e, st["acc"], st["l_done"][u]))
        for hd in range(ATT_HEADS):
            st = state[hd]
            alpha, l_done = list(st["alpha"]), list(st["l_done"])
            for u in range(U):
                st["m"], st["l"], alpha[u] = stage_b(hd, (it + 1) * U + u, (1 - parity) * U + u,
                                                     st["m"], st["l"], st["cmax"][u])
                l_done[u] = st["l"]
            st["alpha"], st["l_done"], st["cmax"] = tuple(alpha), tuple(l_done), tuple(cmax_new[hd])
        for hd, e, acc, l_done in done:
            @pl.when(last_ref[e] != 0)
            def _():
                q0 = pl.multiple_of(qi_ref[e] * TQ, TQ)
                o_ref[pl.ds(q0, TQ), hd * V_HEAD_DIM:(hd + 1) * V_HEAD_DIM] = (
                    (acc / l_done).T.astype(_bf16))
        return tuple(state)

    def two_iterations(i, carry):
        return iteration(2 * i + 1, 1, iteration(2 * i, 0, carry))

    zeros = jnp.zeros((1, TQ), _f32)
    ones = jnp.ones((1, TQ), _f32)
    init = dict(m=zeros, l=zeros, acc=jnp.zeros((V_HEAD_DIM, TQ), _f32),
                alpha=(zeros,) * U, l_done=(ones,) * U, cmax=(zeros,) * U)
    lax.fori_loop(0, n_iter // 2, two_iterations, (init,) * ATT_HEADS)


def _post_kernel(attn_ref, sg_ref, gc_ref, x_ref, w_o_ref, w_out_ref, g_mix_post_ref,
                 g_mlp_pre_ref, w_ff1_ref, w_ff2_ref, g_mlp_post_ref, out_ref):
    y_attn = _dot(attn_ref[...], w_o_ref[...])
    merged = sg_ref[...].astype(_f32) * y_attn + gc_ref[...].astype(_f32)
    m = _dot(merged.astype(_bf16), w_out_ref[...])
    x1 = x_ref[...] + _rms(m, g_mix_post_ref[...])
    h2 = _rms(x1, g_mlp_pre_ref[...]).astype(_bf16)
    f = jnp.zeros(x1.shape, _f32)
    for c in range(D_FF // FF_CHUNK):
        hc = jnp.maximum(_dot(h2, w_ff1_ref[:, c * FF_CHUNK:(c + 1) * FF_CHUNK]), 0.0)
        f = f + _dot((hc * hc).astype(_bf16), w_ff2_ref[c * FF_CHUNK:(c + 1) * FF_CHUNK, :])
    out_ref[...] = x1 + _rms(f, g_mlp_post_ref[...])


def _resident(shape):
    return pl.BlockSpec(shape, lambda *_: (0,) * len(shape), pipeline_mode=pl.Buffered(1))


def _rows(tm, cols):
    return pl.BlockSpec((tm, cols), lambda i: (i, 0))


def kernel(x, positions, norm_mix_pre, w_in, q_norm, w_uq, kv_norm, w_uk, w_uv, w_o_attn,
           conv_w, conv_b, conv_ln_g, conv_ln_b, w_pw2, b_pw2, w_out, norm_mix_post,
           norm_mlp_pre, w_ff1, w_ff2, norm_mlp_post):
    B, S, D = x.shape
    T = B * S
    assert D == D_MODEL and w_in.shape[0] == 1
    assert S % TM_IN == 0 and S % TK == 0 and TK % TQ == 0 and T % TM_POST == 0
    assert N_HEADS % ATT_HEADS == 0
    C = CONV_CHANNELS
    H = N_HEADS
    tiles_per_seq = S // TM_IN

    w_in0 = w_in[0]
    o_q, o_kv, o_kr = Q_LORA_RANK, Q_LORA_RANK + KV_LORA_RANK, Q_LORA_RANK + KV_LORA_RANK + QK_ROPE_DIM
    k1 = w_in0[:, o_kv:o_kv + HALF_ROPE]
    k2 = w_in0[:, o_kv + HALF_ROPE:o_kr]
    w_in_k = jnp.concatenate([w_in0[:, :o_kv], k1, k2, k2, k1, w_in0[:, o_kr:]], axis=1).astype(_bf16)

    wq = w_uq[0].reshape(Q_LORA_RANK, H, QK_NOPE_DIM + QK_ROPE_DIM)
    r1 = wq[:, :, QK_NOPE_DIM:QK_NOPE_DIM + HALF_ROPE]
    r2 = wq[:, :, QK_NOPE_DIM + HALF_ROPE:]
    w_uq_k = jnp.concatenate([wq[:, :, :QK_NOPE_DIM], r1, r2, r2, r1], axis=2)
    w_uq_k = w_uq_k.reshape(Q_LORA_RANK, H * HEAD_SLOT).astype(_bf16)
    w_uk_k = w_uk[0].reshape(KV_LORA_RANK, H * QK_NOPE_DIM).astype(_bf16)
    w_uvt_k = w_uv[0].reshape(KV_LORA_RANK, H * V_HEAD_DIM).T.astype(_bf16)
    cw = jnp.pad(conv_w[0].reshape(CONV_WIDTH, C), ((0, 1), (0, 0)))

    inv_freq = ROPE_THETA ** (-jnp.arange(0, QK_ROPE_DIM, 2, dtype=_f32) / QK_ROPE_DIM)
    invf4 = jnp.tile(inv_freq, 4).reshape(1, ROPE_LANES)

    row = lambda a: a.reshape(1, -1)
    x2 = x.reshape(T, D)
    pos2 = positions.reshape(T, 1)

    q, k, vt, sg, gc = pl.pallas_call(
        functools.partial(_inproj_kernel, tiles_per_seq=tiles_per_seq),
        grid=(T // TM_IN,),
        in_specs=[
            _rows(TM_IN, D), _rows(TM_IN, 1), _resident((1, ROPE_LANES)), _resident((1, D)),
            _resident(w_in_k.shape), _resident((1, Q_LORA_RANK)), _resident(w_uq_k.shape),
            _resident((1, KV_LORA_RANK)), _resident(w_uk_k.shape), _resident(w_uvt_k.shape),
            _resident(cw.shape), _resident((1, C)), _resident((1, C)), _resident((1, C)),
            _resident((C, D)), _resident((1, D)),
        ],
        out_specs=[_rows(TM_IN, H * HEAD_SLOT), _rows(TM_IN, H * HEAD_SLOT),
                   pl.BlockSpec((None, H * V_HEAD_DIM, TM_IN),
                                lambda i: (i // tiles_per_seq, 0, i % tiles_per_seq)),
                   _rows(TM_IN, D), _rows(TM_IN, D)],
        out_shape=[jax.ShapeDtypeStruct((T, H * HEAD_SLOT), _bf16),
                   jax.ShapeDtypeStruct((T, H * HEAD_SLOT), _bf16),
                   jax.ShapeDtypeStruct((B, H * V_HEAD_DIM, S), _bf16),
                   jax.ShapeDtypeStruct((T, D), _bf16),
                   jax.ShapeDtypeStruct((T, D), _bf16)],
        scratch_shapes=[pltpu.VMEM((HALO + TM_IN, C), _f32), pltpu.VMEM((TM_IN, C), _f32),
                        pltpu.VMEM((TM_IN, C), _bf16)],
        compiler_params=pltpu.CompilerParams(dimension_semantics=("arbitrary",),
                                             vmem_limit_bytes=VMEM_LIMIT),
        name="inproj",
    )(x2, pos2, invf4, row(norm_mix_pre[0]), w_in_k, row(q_norm[0]), w_uq_k, row(kv_norm[0]),
      w_uk_k, w_uvt_k, cw, row(conv_b[0]), row(conv_ln_g[0]), row(conv_ln_b[0]),
      w_pw2[0].astype(_bf16), row(b_pw2[0]))

    n_blocks, tables = _attn_stream(S)
    assert n_blocks % (2 * ATT_UNROLL) == 0
    n_iter = n_blocks // ATT_UNROLL + 2
    mask_tiles = _attn_mask_tiles()
    qk_spec = pl.BlockSpec((S, ATT_HEADS * HEAD_SLOT), lambda b, g, *_: (b, g))
    attn = pl.pallas_call(
        functools.partial(_attn_kernel, n_iter=n_iter),
        grid_spec=pltpu.PrefetchScalarGridSpec(
            num_scalar_prefetch=len(tables),
            grid=(B, H // ATT_HEADS),
            in_specs=[qk_spec, qk_spec,
                      pl.BlockSpec((None, ATT_HEADS * V_HEAD_DIM, S), lambda b, g, *_: (b, g, 0)),
                      pl.BlockSpec(mask_tiles.shape, lambda b, g, *_: (0, 0, 0))],
            out_specs=pl.BlockSpec((S, ATT_HEADS * V_HEAD_DIM), lambda b, g, *_: (b, g)),
            scratch_shapes=[pltpu.VMEM((ATT_HEADS, 2 * ATT_UNROLL, TK, TQ), _f32),
                            pltpu.VMEM((ATT_HEADS, 2 * ATT_UNROLL, TK, TQ), _bf16)]),
        out_shape=jax.ShapeDtypeStruct((T, H * V_HEAD_DIM), _bf16),
        compiler_params=pltpu.CompilerParams(dimension_semantics=("arbitrary", "arbitrary"),
                                             vmem_limit_bytes=VMEM_LIMIT),
        name="attn",
    )(*[jnp.asarray(t) for t in tables], q, k, vt, jnp.asarray(mask_tiles))

    out = pl.pallas_call(
        _post_kernel,
        grid=(T // TM_POST,),
        in_specs=[_rows(TM_POST, H * V_HEAD_DIM), _rows(TM_POST, D), _rows(TM_POST, D),
                  _rows(TM_POST, D), _resident((H * V_HEAD_DIM, D)), _resident((D, D)),
                  _resident((1, D)), _resident((1, D)), _resident((D, D_FF)),
                  _resident((D_FF, D)), _resident((1, D))],
        out_specs=_rows(TM_POST, D),
        out_shape=jax.ShapeDtypeStruct((T, D), _f32),
        compiler_params=pltpu.CompilerParams(dimension_semantics=("arbitrary",),
                                             vmem_limit_bytes=VMEM_LIMIT),
        name="post",
    )(attn, sg, gc, x2, w_o_attn[0].astype(_bf16), w_out[0].astype(_bf16),
      row(norm_mix_post[0]), row(norm_mlp_pre[0]), w_ff1[0].astype(_bf16),
      w_ff2[0].astype(_bf16), row(norm_mlp_post[0]))
    return out.reshape(B, S, D)
```

```python
import functools
import math

import jax
import jax.numpy as jnp
import numpy as np
from jax import lax
from jax.experimental import pallas as pl
from jax.experimental.pallas import tpu as pltpu

D_MODEL = 1024
N_HEADS = 8
QK_NOPE_DIM = 128
QK_ROPE_DIM = 64
V_HEAD_DIM = 128
Q_LORA_RANK = 384
KV_LORA_RANK = 256
ROPE_THETA = 10000.0
CONV_CHANNELS = 1024
CONV_WIDTH = 31
D_FF = 4096
EPS = 1e-6

HALF_ROPE = QK_ROPE_DIM // 2
ROPE_LANES = 4 * HALF_ROPE
HEAD_SLOT = QK_NOPE_DIM + ROPE_LANES
SMALL_COLS = Q_LORA_RANK + KV_LORA_RANK + ROPE_LANES
HALO = 32
SUBLANES = 8
LANES = 128
CONV_ROWS = 64

Q_SCALE = (QK_NOPE_DIM + QK_ROPE_DIM) ** -0.5 * math.log2(math.e)
MASK_VALUE = -1e30

TM_IN = 512
TM_POST = 512
TQ = 256
TK = 512
ATT_HEADS = 2
ATT_UNROLL = 2
SUM_ROWS = 16
FF_CHUNK = 1024

VMEM_LIMIT = 60 * 1024 * 1024

_bf16 = jnp.bfloat16
_f32 = jnp.float32
_NT = (((1,), (1,)), ((), ()))


def _dot(a, b):
    return jnp.dot(a, b, preferred_element_type=_f32)


def _rms(x, g):
    return x * lax.rsqrt(jnp.mean(x * x, axis=-1, keepdims=True) + EPS) * g


def _inproj_kernel(x_ref, pos_ref, invf_ref, g_pre_ref, w_in_ref, qn_g_ref, w_uq_ref,
                   kvn_g_ref, w_uk_ref, w_uvt_ref, cw_ref, cb_ref, lng_ref, lnb_ref,
                   w_pw2_ref, b_pw2_ref,
                   q_ref, k_ref, vt_ref, sg_ref, gc_ref,
                   ubuf, cacc, *, tiles_per_seq):
    tm = x_ref.shape[0]
    h = _rms(x_ref[...], g_pre_ref[...]).astype(_bf16)

    ang = pos_ref[...].astype(_f32) * invf_ref[...]
    lane = lax.broadcasted_iota(jnp.int32, ang.shape, 1)
    sin = jnp.sin(ang)
    rope4 = jnp.where(lane < 2 * HALF_ROPE, jnp.cos(ang),
                      jnp.where(lane < 3 * HALF_ROPE, -sin, sin))

    zs = _dot(h, w_in_ref[:, 0:SMALL_COLS])
    cq = zs[:, 0:Q_LORA_RANK]
    ckv = zs[:, Q_LORA_RANK:Q_LORA_RANK + KV_LORA_RANK]
    kr4 = zs[:, Q_LORA_RANK + KV_LORA_RANK:SMALL_COLS]

    q = _dot(_rms(cq, qn_g_ref[...]).astype(_bf16), w_uq_ref[...])
    rope4_q = rope4 * Q_SCALE
    for hd in range(N_HEADS):
        lo = hd * HEAD_SLOT
        q_ref[:, lo:lo + QK_NOPE_DIM] = (q[:, lo:lo + QK_NOPE_DIM] * Q_SCALE).astype(_bf16)
        q_ref[:, lo + QK_NOPE_DIM:lo + HEAD_SLOT] = (
            q[:, lo + QK_NOPE_DIM:lo + HEAD_SLOT] * rope4_q).astype(_bf16)

    kp = kr4 * rope4
    kdup = (kp + pltpu.roll(kp, 2 * HALF_ROPE, 1)).astype(_bf16)
    ckvn = _rms(ckv, kvn_g_ref[...]).astype(_bf16)
    kn = _dot(ckvn, w_uk_ref[...])
    for hd in range(N_HEADS):
        lo = hd * HEAD_SLOT
        k_ref[:, lo:lo + QK_NOPE_DIM] = kn[:, hd * QK_NOPE_DIM:(hd + 1) * QK_NOPE_DIM].astype(_bf16)
        k_ref[:, lo + QK_NOPE_DIM:lo + HEAD_SLOT] = kdup
    vt_ref[...] = lax.dot_general(w_uvt_ref[...], ckvn, _NT,
                                  preferred_element_type=_f32).astype(_bf16)

    c0 = SMALL_COLS
    C = CONV_CHANNELS
    sg_ref[...] = jax.nn.sigmoid(_dot(h, w_in_ref[:, c0 + 2 * C:c0 + 3 * C])).astype(_bf16)

    @pl.when(pl.program_id(0) % tiles_per_seq == 0)
    def _():
        ubuf[0:HALO, :] = jnp.zeros((HALO, C), _f32)

    a = _dot(h, w_in_ref[:, c0:c0 + C])
    b = _dot(h, w_in_ref[:, c0 + C:c0 + 2 * C])
    ubuf[HALO:HALO + tm, :] = a * jax.nn.sigmoid(b)

    shift = HALO - (CONV_WIDTH - 1)
    win = CONV_ROWS + HALO

    def conv_rows(r, carry):
        r0 = pl.multiple_of(r * CONV_ROWS, CONV_ROWS)

        def conv_lanes(c, carry):
            l0 = pl.multiple_of(c * LANES, LANES)
            window = ubuf[pl.ds(r0, win), pl.ds(l0, LANES)]
            acc = jnp.broadcast_to(cb_ref[:, pl.ds(l0, LANES)], (CONV_ROWS, LANES))
            for phase in range(SUBLANES):
                rolled = window if phase == 0 else pltpu.roll(window, win - phase, 0)
                for a in range(win // SUBLANES):
                    t = SUBLANES * a + phase - shift
                    if 0 <= t < CONV_WIDTH:
                        acc = acc + (rolled[SUBLANES * a:SUBLANES * a + CONV_ROWS, :]
                                     * cw_ref[pl.ds(t, 1), pl.ds(l0, LANES)])
            cacc[pl.ds(r0, CONV_ROWS), pl.ds(l0, LANES)] = acc
            return carry

        return lax.fori_loop(0, C // LANES, conv_lanes, carry)

    lax.fori_loop(0, tm // CONV_ROWS, conv_rows, 0)
    ubuf[0:HALO, :] = ubuf[tm:tm + HALO, :]

    conv = cacc[...]
    xc = conv - jnp.mean(conv, axis=-1, keepdims=True)
    var = jnp.mean(xc * xc, axis=-1, keepdims=True)
    y = xc * lax.rsqrt(var + EPS) * lng_ref[...] + lnb_ref[...]
    y_conv = _dot((y * jax.nn.sigmoid(y)).astype(_bf16), w_pw2_ref[...]) + b_pw2_ref[...]
    g_conv = jax.nn.sigmoid(_dot(h, w_in_ref[:, c0 + 3 * C:c0 + 4 * C]))
    gc_ref[...] = (g_conv * y_conv).astype(_bf16)


def _attn_stream(seq):
    qi, kj, first, last, kind = [], [], [], [], []
    for i in range(seq // TQ):
        n_blocks = (i * TQ + TQ - 1) // TK + 1
        for j in range(n_blocks):
            qi.append(i)
            kj.append(j)
            first.append(int(j == 0))
            last.append(int(j == n_blocks - 1))
            kind.append(0 if j < n_blocks - 1 else 1 + (i * TQ - j * TK) // TQ)
    pad = [0] * (2 * ATT_UNROLL)
    tables = [np.asarray(pad + t + pad, np.int32) for t in (qi, kj, first, last, kind)]
    return len(qi), tables


def _attn_mask_tiles():
    kk = np.arange(TK)[:, None]
    qq = np.arange(TQ)[None, :]
    tiles = [np.zeros((TK, TQ), np.float32)]
    for d in range(TK // TQ):
        tiles.append(np.where(kk <= qq + d * TQ, 0.0, MASK_VALUE).astype(np.float32))
    return np.stack(tiles)


def _attn_kernel(qi_ref, kj_ref, first_ref, last_ref, kind_ref,
                 q_ref, k_ref, vt_ref, mask_ref, o_ref, s_buf, p_buf, *, n_iter):
    U = ATT_UNROLL
    s_buf[...] = jnp.zeros(s_buf.shape, _f32)
    p_buf[...] = jnp.zeros(p_buf.shape, _bf16)

    def head_cols(ref, hd, width, rows):
        return ref[rows, hd * width:(hd + 1) * width]

    def stage_a(hd, e, slot):
        q0 = pl.multiple_of(qi_ref[e] * TQ, TQ)
        k0 = pl.multiple_of(kj_ref[e] * TK, TK)
        s = lax.dot_general(head_cols(k_ref, hd, HEAD_SLOT, pl.ds(k0, TK)),
                            head_cols(q_ref, hd, HEAD_SLOT, pl.ds(q0, TQ)), _NT,
                            preferred_element_type=_f32)
        s = s + mask_ref[kind_ref[e]]
        s_buf[hd, slot] = s
        return jnp.max(s, axis=0, keepdims=True)

    def stage_b(hd, e, slot, m, cmax):
        m_prev = jnp.where(first_ref[e] != 0, -jnp.inf, m)
        m_new = jnp.maximum(m_prev, cmax)
        p_buf[hd, slot] = jnp.exp2(s_buf[hd, slot] - m_new).astype(_bf16)
        return m_new, jnp.exp2(m_prev - m_new)

    ones_rows = jnp.ones((SUM_ROWS, TK), _bf16)

    def stage_c(hd, e, slot, acc, l, alpha):
        k0 = pl.multiple_of(kj_ref[e] * TK, TK)
        vt = vt_ref[hd * V_HEAD_DIM:(hd + 1) * V_HEAD_DIM, pl.ds(k0, TK)]
        pv = _dot(jnp.concatenate([vt, ones_rows], axis=0), p_buf[hd, slot])
        return alpha * acc + pv[:V_HEAD_DIM], alpha * l + pv[V_HEAD_DIM:V_HEAD_DIM + 1]

    def iteration(it, parity, carry):
        state = [dict(c) for c in carry]
        alpha_new = [list(st["alpha"]) for st in state]
        for u in range(U):
            for hd in range(ATT_HEADS):
                st = state[hd]
                st["m"], alpha_new[hd][u] = stage_b(hd, (it + 1) * U + u, (1 - parity) * U + u,
                                                    st["m"], st["cmax"][u])
        cmax_new = [[None] * U for _ in range(ATT_HEADS)]
        for u in range(U):
            for hd in range(ATT_HEADS):
                cmax_new[hd][u] = stage_a(hd, (it + 2) * U + u, parity * U + u)
        done = []
        for u in range(U):
            for hd in range(ATT_HEADS):
                st = state[hd]
                e = it * U + u
                st["acc"], st["l"] = stage_c(hd, e, parity * U + u, st["acc"], st["l"],
                                             st["alpha"][u])
                done.append((hd, e, st["acc"], st["l"]))
        for hd in range(ATT_HEADS):
            state[hd]["alpha"] = tuple(alpha_new[hd])
            state[hd]["cmax"] = tuple(cmax_new[hd])
        for hd, e, acc, l in done:
            @pl.when(last_ref[e] != 0)
            def _():
                q0 = pl.multiple_of(qi_ref[e] * TQ, TQ)
                o_ref[pl.ds(q0, TQ), hd * V_HEAD_DIM:(hd + 1) * V_HEAD_DIM] = (
                    (acc / l).T.astype(_bf16))
        return tuple(state)

    def two_iterations(i, carry):
        return iteration(2 * i + 1, 1, iteration(2 * i, 0, carry))

    zeros = jnp.zeros((1, TQ), _f32)
    init = dict(m=zeros, l=zeros, acc=jnp.zeros((V_HEAD_DIM, TQ), _f32),
                alpha=(zeros,) * U, cmax=(zeros,) * U)
    lax.fori_loop(0, n_iter // 2, two_iterations, (init,) * ATT_HEADS)


def _post_kernel(attn_ref, sg_ref, gc_ref, x_ref, w_o_ref, w_out_ref, g_mix_post_ref,
                 g_mlp_pre_ref, w_ff1_ref, w_ff2_ref, g_mlp_post_ref, out_ref):
    y_attn = _dot(attn_ref[...], w_o_ref[...])
    merged = sg_ref[...].astype(_f32) * y_attn + gc_ref[...].astype(_f32)
    m = _dot(merged.astype(_bf16), w_out_ref[...])
    x1 = x_ref[...] + _rms(m, g_mix_post_ref[...])
    h2 = _rms(x1, g_mlp_pre_ref[...]).astype(_bf16)
    f = jnp.zeros(x1.shape, _f32)
    for c in range(D_FF // FF_CHUNK):
        hc = jnp.maximum(_dot(h2, w_ff1_ref[:, c * FF_CHUNK:(c + 1) * FF_CHUNK]), 0.0)
        f = f + _dot((hc * hc).astype(_bf16), w_ff2_ref[c * FF_CHUNK:(c + 1) * FF_CHUNK, :])
    out_ref[...] = x1 + _rms(f, g_mlp_post_ref[...])


def _resident(shape):
    return pl.BlockSpec(shape, lambda *_: (0,) * len(shape), pipeline_mode=pl.Buffered(1))


def _rows(tm, cols):
    return pl.BlockSpec((tm, cols), lambda i: (i, 0))


def kernel(x, positions, norm_mix_pre, w_in, q_norm, w_uq, kv_norm, w_uk, w_uv, w_o_attn,
           conv_w, conv_b, conv_ln_g, conv_ln_b, w_pw2, b_pw2, w_out, norm_mix_post,
           norm_mlp_pre, w_ff1, w_ff2, norm_mlp_post):
    B, S, D = x.shape
    T = B * S
    assert D == D_MODEL and w_in.shape[0] == 1
    assert S % TM_IN == 0 and S % TK == 0 and TK % TQ == 0 and T % TM_POST == 0
    assert N_HEADS % ATT_HEADS == 0
    C = CONV_CHANNELS
    H = N_HEADS
    tiles_per_seq = S // TM_IN

    w_in0 = w_in[0]
    o_q, o_kv, o_kr = Q_LORA_RANK, Q_LORA_RANK + KV_LORA_RANK, Q_LORA_RANK + KV_LORA_RANK + QK_ROPE_DIM
    k1 = w_in0[:, o_kv:o_kv + HALF_ROPE]
    k2 = w_in0[:, o_kv + HALF_ROPE:o_kr]
    w_in_k = jnp.concatenate([w_in0[:, :o_kv], k1, k2, k2, k1, w_in0[:, o_kr:]], axis=1).astype(_bf16)

    wq = w_uq[0].reshape(Q_LORA_RANK, H, QK_NOPE_DIM + QK_ROPE_DIM)
    r1 = wq[:, :, QK_NOPE_DIM:QK_NOPE_DIM + HALF_ROPE]
    r2 = wq[:, :, QK_NOPE_DIM + HALF_ROPE:]
    w_uq_k = jnp.concatenate([wq[:, :, :QK_NOPE_DIM], r1, r2, r2, r1], axis=2)
    w_uq_k = w_uq_k.reshape(Q_LORA_RANK, H * HEAD_SLOT).astype(_bf16)
    w_uk_k = w_uk[0].reshape(KV_LORA_RANK, H * QK_NOPE_DIM).astype(_bf16)
    w_uvt_k = w_uv[0].reshape(KV_LORA_RANK, H * V_HEAD_DIM).T.astype(_bf16)
    cw = jnp.pad(conv_w[0].reshape(CONV_WIDTH, C), ((0, 1), (0, 0)))

    inv_freq = ROPE_THETA ** (-jnp.arange(0, QK_ROPE_DIM, 2, dtype=_f32) / QK_ROPE_DIM)
    invf4 = jnp.tile(inv_freq, 4).reshape(1, ROPE_LANES)

    row = lambda a: a.reshape(1, -1)
    x2 = x.reshape(T, D)
    pos2 = positions.reshape(T, 1)

    q, k, vt, sg, gc = pl.pallas_call(
        functools.partial(_inproj_kernel, tiles_per_seq=tiles_per_seq),
        grid=(T // TM_IN,),
        in_specs=[
            _rows(TM_IN, D), _rows(TM_IN, 1), _resident((1, ROPE_LANES)), _resident((1, D)),
            _resident(w_in_k.shape), _resident((1, Q_LORA_RANK)), _resident(w_uq_k.shape),
            _resident((1, KV_LORA_RANK)), _resident(w_uk_k.shape), _resident(w_uvt_k.shape),
            _resident(cw.shape), _resident((1, C)), _resident((1, C)), _resident((1, C)),
            _resident((C, D)), _resident((1, D)),
        ],
        out_specs=[_rows(TM_IN, H * HEAD_SLOT), _rows(TM_IN, H * HEAD_SLOT),
                   pl.BlockSpec((None, H * V_HEAD_DIM, TM_IN),
                                lambda i: (i // tiles_per_seq, 0, i % tiles_per_seq)),
                   _rows(TM_IN, D), _rows(TM_IN, D)],
        out_shape=[jax.ShapeDtypeStruct((T, H * HEAD_SLOT), _bf16),
                   jax.ShapeDtypeStruct((T, H * HEAD_SLOT), _bf16),
                   jax.ShapeDtypeStruct((B, H * V_HEAD_DIM, S), _bf16),
                   jax.ShapeDtypeStruct((T, D), _bf16),
                   jax.ShapeDtypeStruct((T, D), _bf16)],
        scratch_shapes=[pltpu.VMEM((HALO + TM_IN, C), _f32), pltpu.VMEM((TM_IN, C), _f32)],
        compiler_params=pltpu.CompilerParams(dimension_semantics=("arbitrary",),
                                             vmem_limit_bytes=VMEM_LIMIT),
        name="inproj",
    )(x2, pos2, invf4, row(norm_mix_pre[0]), w_in_k, row(q_norm[0]), w_uq_k, row(kv_norm[0]),
      w_uk_k, w_uvt_k, cw, row(conv_b[0]), row(conv_ln_g[0]), row(conv_ln_b[0]),
      w_pw2[0].astype(_bf16), row(b_pw2[0]))

    n_blocks, tables = _attn_stream(S)
    assert n_blocks % (2 * ATT_UNROLL) == 0
    n_iter = n_blocks // ATT_UNROLL + 2
    mask_tiles = _attn_mask_tiles()
    qk_spec = pl.BlockSpec((S, ATT_HEADS * HEAD_SLOT), lambda b, g, *_: (b, g))
    attn = pl.pallas_call(
        functools.partial(_attn_kernel, n_iter=n_iter),
        grid_spec=pltpu.PrefetchScalarGridSpec(
            num_scalar_prefetch=len(tables),
            grid=(B, H // ATT_HEADS),
            in_specs=[qk_spec, qk_spec,
                      pl.BlockSpec((None, ATT_HEADS * V_HEAD_DIM, S), lambda b, g, *_: (b, g, 0)),
                      pl.BlockSpec(mask_tiles.shape, lambda b, g, *_: (0, 0, 0))],
            out_specs=pl.BlockSpec((S, ATT_HEADS * V_HEAD_DIM), lambda b, g, *_: (b, g)),
            scratch_shapes=[pltpu.VMEM((ATT_HEADS, 2 * ATT_UNROLL, TK, TQ), _f32),
                            pltpu.VMEM((ATT_HEADS, 2 * ATT_UNROLL, TK, TQ), _bf16)]),
        out_shape=jax.ShapeDtypeStruct((T, H * V_HEAD_DIM), _bf16),
        compiler_params=pltpu.CompilerParams(dimension_semantics=("arbitrary", "arbitrary"),
                                             vmem_limit_bytes=VMEM_LIMIT),
        name="attn",
    )(*[jnp.asarray(t) for t in tables], q, k, vt, jnp.asarray(mask_tiles))

    out = pl.pallas_call(
        _post_kernel,
        grid=(T // TM_POST,),
        in_specs=[_rows(TM_POST, H * V_HEAD_DIM), _rows(TM_POST, D), _rows(TM_POST, D),
                  _rows(TM_POST, D), _resident((H * V_HEAD_DIM, D)), _resident((D, D)),
                  _resident((1, D)), _resident((1, D)), _resident((D, D_FF)),
                  _resident((D_FF, D)), _resident((1, D))],
        out_specs=_rows(TM_POST, D),
        out_shape=jax.ShapeDtypeStruct((T, D), _f32),
        compiler_params=pltpu.CompilerParams(dimension_semantics=("arbitrary",),
                                             vmem_limit_bytes=VMEM_LIMIT),
        name="post",
    )(attn, sg, gc, x2, w_o_attn[0].astype(_bf16), w_out[0].astype(_bf16),
      row(norm_mix_post[0]), row(norm_mlp_pre[0]), w_ff1[0].astype(_bf16),
      w_ff2[0].astype(_bf16), row(norm_mlp_post[0]))
    return out.reshape(B, S, D)
```

```python
import functools
import math

import jax
import jax.numpy as jnp
import numpy as np
from jax import lax
from jax.experimental import pallas as pl
from jax.experimental.pallas import tpu as pltpu

D_MODEL = 1024
N_HEADS = 8
QK_NOPE_DIM = 128
QK_ROPE_DIM = 64
V_HEAD_DIM = 128
Q_LORA_RANK = 384
KV_LORA_RANK = 256
ROPE_THETA = 10000.0
CONV_CHANNELS = 1024
CONV_WIDTH = 31
D_FF = 4096
EPS = 1e-6

HALF_ROPE = QK_ROPE_DIM // 2
ROPE_LANES = 4 * HALF_ROPE
HEAD_SLOT = QK_NOPE_DIM + ROPE_LANES
SMALL_COLS = Q_LORA_RANK + KV_LORA_RANK + ROPE_LANES
HALO = 32
SUBLANES = 8
LANES = 128
CONV_ROWS = 64

Q_SCALE = (QK_NOPE_DIM + QK_ROPE_DIM) ** -0.5 * math.log2(math.e)
MASK_VALUE = -1e30

TM_IN = 512
TM_POST = 512
TQ = 256
TK = 512
ATT_HEADS = 2
ATT_UNROLL = 2
SUM_ROWS = 16
FF_CHUNK = 1024

VMEM_LIMIT = 60 * 1024 * 1024

_bf16 = jnp.bfloat16
_f32 = jnp.float32
_NT = (((1,), (1,)), ((), ()))


def _dot(a, b):
    return jnp.dot(a, b, preferred_element_type=_f32)


def _rms(x, g):
    return x * lax.rsqrt(jnp.mean(x * x, axis=-1, keepdims=True) + EPS) * g


def _inproj_kernel(x_ref, pos_ref, invf_ref, g_pre_ref, w_small_ref, w_big_ref, qn_g_ref, w_uq_ref,
                   kvn_g_ref, w_uk_ref, w_uvt_ref, cw_ref, cb_ref, lng_ref, lnb_ref,
                   w_pw2_ref, b_pw2_ref,
                   q_ref, k_ref, vt_ref, sg_ref, gc_ref,
                   ubuf, cacc, hbuf, *, tiles_per_seq):
    tm = x_ref.shape[0]
    C = CONV_CHANNELS

    @pl.when(pl.program_id(0) % tiles_per_seq == 0)
    def _():
        ubuf[0:HALO, :] = jnp.zeros((HALO, C), _f32)

    hbuf[...] = _rms(x_ref[...], g_pre_ref[...]).astype(_bf16)
    a = _dot(hbuf[...], w_big_ref[:, 0:C])
    b = _dot(hbuf[...], w_big_ref[:, C:2 * C])
    ubuf[HALO:HALO + tm, :] = a * jax.nn.sigmoid(b)

    shift = HALO - (CONV_WIDTH - 1)
    win = CONV_ROWS + HALO

    def conv_rows(r, carry):
        r0 = pl.multiple_of(r * CONV_ROWS, CONV_ROWS)

        def conv_lanes(c, carry):
            l0 = pl.multiple_of(c * LANES, LANES)
            window = ubuf[pl.ds(r0, win), pl.ds(l0, LANES)]
            acc = jnp.broadcast_to(cb_ref[:, pl.ds(l0, LANES)], (CONV_ROWS, LANES))
            for phase in range(SUBLANES):
                rolled = window if phase == 0 else pltpu.roll(window, win - phase, 0)
                for a in range(win // SUBLANES):
                    t = SUBLANES * a + phase - shift
                    if 0 <= t < CONV_WIDTH:
                        acc = acc + (rolled[SUBLANES * a:SUBLANES * a + CONV_ROWS, :]
                                     * cw_ref[pl.ds(t, 1), pl.ds(l0, LANES)])
            cacc[pl.ds(r0, CONV_ROWS), pl.ds(l0, LANES)] = acc
            return carry

        return lax.fori_loop(0, C // LANES, conv_lanes, carry)

    lax.fori_loop(0, tm // CONV_ROWS, conv_rows, 0)
    ubuf[0:HALO, :] = ubuf[tm:tm + HALO, :]

    conv = cacc[...]
    xc = conv - jnp.mean(conv, axis=-1, keepdims=True)
    var = jnp.mean(xc * xc, axis=-1, keepdims=True)
    y = xc * lax.rsqrt(var + EPS) * lng_ref[...] + lnb_ref[...]
    swish = (y * jax.nn.sigmoid(y)).astype(_bf16)

    h = hbuf[...]
    sg_ref[...] = jax.nn.sigmoid(_dot(h, w_big_ref[:, 2 * C:3 * C])).astype(_bf16)
    g_conv = jax.nn.sigmoid(_dot(h, w_big_ref[:, 3 * C:4 * C]))

    zs = _dot(h, w_small_ref[...])
    cq = zs[:, 0:Q_LORA_RANK]
    ckv = zs[:, Q_LORA_RANK:Q_LORA_RANK + KV_LORA_RANK]
    kr4 = zs[:, Q_LORA_RANK + KV_LORA_RANK:SMALL_COLS]

    ang = pos_ref[...].astype(_f32) * invf_ref[...]
    lane = lax.broadcasted_iota(jnp.int32, ang.shape, 1)
    sin = jnp.sin(ang)
    rope4 = jnp.where(lane < 2 * HALF_ROPE, jnp.cos(ang),
                      jnp.where(lane < 3 * HALF_ROPE, -sin, sin))

    q = _dot(_rms(cq, qn_g_ref[...]).astype(_bf16), w_uq_ref[...])
    rope4_q = rope4 * Q_SCALE
    for hd in range(N_HEADS):
        lo = hd * HEAD_SLOT
        q_ref[:, lo:lo + QK_NOPE_DIM] = (q[:, lo:lo + QK_NOPE_DIM] * Q_SCALE).astype(_bf16)
        q_ref[:, lo + QK_NOPE_DIM:lo + HEAD_SLOT] = (
            q[:, lo + QK_NOPE_DIM:lo + HEAD_SLOT] * rope4_q).astype(_bf16)

    kp = kr4 * rope4
    kdup = (kp + pltpu.roll(kp, 2 * HALF_ROPE, 1)).astype(_bf16)
    ckvn = _rms(ckv, kvn_g_ref[...]).astype(_bf16)
    kn = _dot(ckvn, w_uk_ref[...])
    for hd in range(N_HEADS):
        lo = hd * HEAD_SLOT
        k_ref[:, lo:lo + QK_NOPE_DIM] = kn[:, hd * QK_NOPE_DIM:(hd + 1) * QK_NOPE_DIM].astype(_bf16)
        k_ref[:, lo + QK_NOPE_DIM:lo + HEAD_SLOT] = kdup
    vt_ref[...] = lax.dot_general(w_uvt_ref[...], ckvn, _NT,
                                  preferred_element_type=_f32).astype(_bf16)

    y_conv = _dot(swish, w_pw2_ref[...]) + b_pw2_ref[...]
    gc_ref[...] = (g_conv * y_conv).astype(_bf16)


def _attn_stream(seq):
    qi, kj, first, last, kind = [], [], [], [], []
    for i in range(seq // TQ):
        n_blocks = (i * TQ + TQ - 1) // TK + 1
        for j in range(n_blocks):
            qi.append(i)
            kj.append(j)
            first.append(int(j == 0))
            last.append(int(j == n_blocks - 1))
            kind.append(0 if j < n_blocks - 1 else 1 + (i * TQ - j * TK) // TQ)
    pad = [0] * (2 * ATT_UNROLL)
    tables = [np.asarray(pad + t + pad, np.int32) for t in (qi, kj, first, last, kind)]
    return len(qi), tables


def _attn_mask_tiles():
    kk = np.arange(TK)[:, None]
    qq = np.arange(TQ)[None, :]
    tiles = [np.zeros((TK, TQ), np.float32)]
    for d in range(TK // TQ):
        tiles.append(np.where(kk <= qq + d * TQ, 0.0, MASK_VALUE).astype(np.float32))
    return np.stack(tiles)


def _attn_kernel(qi_ref, kj_ref, first_ref, last_ref, kind_ref,
                 q_ref, k_ref, vt_ref, mask_ref, o_ref, s_buf, p_buf, *, n_iter):
    U = ATT_UNROLL
    s_buf[...] = jnp.zeros(s_buf.shape, _f32)
    p_buf[...] = jnp.zeros(p_buf.shape, _bf16)

    def head_cols(ref, hd, width, rows):
        return ref[rows, hd * width:(hd + 1) * width]

    def stage_a(hd, e, slot):
        q0 = pl.multiple_of(qi_ref[e] * TQ, TQ)
        k0 = pl.multiple_of(kj_ref[e] * TK, TK)
        s = lax.dot_general(head_cols(k_ref, hd, HEAD_SLOT, pl.ds(k0, TK)),
                            head_cols(q_ref, hd, HEAD_SLOT, pl.ds(q0, TQ)), _NT,
                            preferred_element_type=_f32)
        s = s + mask_ref[kind_ref[e]]
        s_buf[hd, slot] = s
        return jnp.max(s, axis=0, keepdims=True)

    def stage_b(hd, e, slot, m, cmax):
        m_prev = jnp.where(first_ref[e] != 0, -jnp.inf, m)
        m_new = jnp.maximum(m_prev, cmax)
        p_buf[hd, slot] = jnp.exp2(s_buf[hd, slot] - m_new).astype(_bf16)
        return m_new, jnp.exp2(m_prev - m_new)

    ones_rows = jnp.ones((SUM_ROWS, TK), _bf16)

    def stage_c(hd, e, slot, acc, l, alpha):
        k0 = pl.multiple_of(kj_ref[e] * TK, TK)
        vt = vt_ref[hd * V_HEAD_DIM:(hd + 1) * V_HEAD_DIM, pl.ds(k0, TK)]
        pv = _dot(jnp.concatenate([vt, ones_rows], axis=0), p_buf[hd, slot])
        return alpha * acc + pv[:V_HEAD_DIM], alpha * l + pv[V_HEAD_DIM:V_HEAD_DIM + 1]

    def iteration(it, parity, carry):
        state = [dict(c) for c in carry]
        alpha_new = [list(st["alpha"]) for st in state]
        for u in range(U):
            for hd in range(ATT_HEADS):
                st = state[hd]
                st["m"], alpha_new[hd][u] = stage_b(hd, (it + 1) * U + u, (1 - parity) * U + u,
                                                    st["m"], st["cmax"][u])
        cmax_new = [[None] * U for _ in range(ATT_HEADS)]
        for u in range(U):
            for hd in range(ATT_HEADS):
                cmax_new[hd][u] = stage_a(hd, (it + 2) * U + u, parity * U + u)
        done = []
        for u in range(U):
            for hd in range(ATT_HEADS):
                st = state[hd]
                e = it * U + u
                st["acc"], st["l"] = stage_c(hd, e, parity * U + u, st["acc"], st["l"],
                                             st["alpha"][u])
                done.append((hd, e, st["acc"], st["l"]))
        for hd in range(ATT_HEADS):
            state[hd]["alpha"] = tuple(alpha_new[hd])
            state[hd]["cmax"] = tuple(cmax_new[hd])
        for hd, e, acc, l in done:
            @pl.when(last_ref[e] != 0)
            def _():
                q0 = pl.multiple_of(qi_ref[e] * TQ, TQ)
                o_ref[pl.ds(q0, TQ), hd * V_HEAD_DIM:(hd + 1) * V_HEAD_DIM] = (
                    (acc / l).T.astype(_bf16))
        return tuple(state)

    def two_iterations(i, carry):
        return iteration(2 * i + 1, 1, iteration(2 * i, 0, carry))

    zeros = jnp.zeros((1, TQ), _f32)
    init = dict(m=zeros, l=zeros, acc=jnp.zeros((V_HEAD_DIM, TQ), _f32),
                alpha=(zeros,) * U, cmax=(zeros,) * U)
    lax.fori_loop(0, n_iter // 2, two_iterations, (init,) * ATT_HEADS)


def _post_kernel(attn_ref, sg_ref, gc_ref, x_ref, w_o_ref, w_out_ref, g_mix_post_ref,
                 g_mlp_pre_ref, w_ff1_ref, w_ff2_ref, g_mlp_post_ref, out_ref):
    y_attn = _dot(attn_ref[...], w_o_ref[...])
    merged = sg_ref[...].astype(_f32) * y_attn + gc_ref[...].astype(_f32)
    m = _dot(merged.astype(_bf16), w_out_ref[...])
    x1 = x_ref[...] + _rms(m, g_mix_post_ref[...])
    h2 = _rms(x1, g_mlp_pre_ref[...]).astype(_bf16)
    f = jnp.zeros(x1.shape, _f32)
    for c in range(D_FF // FF_CHUNK):
        hc = jnp.maximum(_dot(h2, w_ff1_ref[:, c * FF_CHUNK:(c + 1) * FF_CHUNK]), 0.0)
        f = f + _dot((hc * hc).astype(_bf16), w_ff2_ref[c * FF_CHUNK:(c + 1) * FF_CHUNK, :])
    out_ref[...] = x1 + _rms(f, g_mlp_post_ref[...])


def _resident(shape):
    return pl.BlockSpec(shape, lambda *_: (0,) * len(shape), pipeline_mode=pl.Buffered(1))


def _rows(tm, cols):
    return pl.BlockSpec((tm, cols), lambda i: (i, 0))


def kernel(x, positions, norm_mix_pre, w_in, q_norm, w_uq, kv_norm, w_uk, w_uv, w_o_attn,
           conv_w, conv_b, conv_ln_g, conv_ln_b, w_pw2, b_pw2, w_out, norm_mix_post,
           norm_mlp_pre, w_ff1, w_ff2, norm_mlp_post):
    B, S, D = x.shape
    T = B * S
    assert D == D_MODEL and w_in.shape[0] == 1
    assert S % TM_IN == 0 and S % TK == 0 and TK % TQ == 0 and T % TM_POST == 0
    assert N_HEADS % ATT_HEADS == 0
    C = CONV_CHANNELS
    H = N_HEADS
    tiles_per_seq = S // TM_IN

    w_in0 = w_in[0]
    o_kv, o_kr = Q_LORA_RANK + KV_LORA_RANK, Q_LORA_RANK + KV_LORA_RANK + QK_ROPE_DIM
    k1 = w_in0[:, o_kv:o_kv + HALF_ROPE]
    k2 = w_in0[:, o_kv + HALF_ROPE:o_kr]
    w_small = jnp.concatenate([w_in0[:, :o_kv], k1, k2, k2, k1], axis=1).astype(_bf16)
    w_big = w_in0[:, o_kr:].astype(_bf16)

    wq = w_uq[0].reshape(Q_LORA_RANK, H, QK_NOPE_DIM + QK_ROPE_DIM)
    r1 = wq[:, :, QK_NOPE_DIM:QK_NOPE_DIM + HALF_ROPE]
    r2 = wq[:, :, QK_NOPE_DIM + HALF_ROPE:]
    w_uq_k = jnp.concatenate([wq[:, :, :QK_NOPE_DIM], r1, r2, r2, r1], axis=2)
    w_uq_k = w_uq_k.reshape(Q_LORA_RANK, H * HEAD_SLOT).astype(_bf16)
    w_uk_k = w_uk[0].reshape(KV_LORA_RANK, H * QK_NOPE_DIM).astype(_bf16)
    w_uvt_k = w_uv[0].reshape(KV_LORA_RANK, H * V_HEAD_DIM).T.astype(_bf16)
    cw = jnp.pad(conv_w[0].reshape(CONV_WIDTH, C), ((0, 1), (0, 0)))

    inv_freq = ROPE_THETA ** (-jnp.arange(0, QK_ROPE_DIM, 2, dtype=_f32) / QK_ROPE_DIM)
    invf4 = jnp.tile(inv_freq, 4).reshape(1, ROPE_LANES)

    row = lambda a: a.reshape(1, -1)
    x2 = x.reshape(T, D)
    pos2 = positions.reshape(T, 1)

    q, k, vt, sg, gc = pl.pallas_call(
        functools.partial(_inproj_kernel, tiles_per_seq=tiles_per_seq),
        grid=(T // TM_IN,),
        in_specs=[
            _rows(TM_IN, D), _rows(TM_IN, 1), _resident((1, ROPE_LANES)), _resident((1, D)),
            _resident(w_small.shape), _resident(w_big.shape),
            _resident((1, Q_LORA_RANK)), _resident(w_uq_k.shape),
            _resident((1, KV_LORA_RANK)), _resident(w_uk_k.shape), _resident(w_uvt_k.shape),
            _resident(cw.shape), _resident((1, C)), _resident((1, C)), _resident((1, C)),
            _resident((C, D)), _resident((1, D)),
        ],
        out_specs=[_rows(TM_IN, H * HEAD_SLOT), _rows(TM_IN, H * HEAD_SLOT),
                   pl.BlockSpec((None, H * V_HEAD_DIM, TM_IN),
                                lambda i: (i // tiles_per_seq, 0, i % tiles_per_seq)),
                   _rows(TM_IN, D), _rows(TM_IN, D)],
        out_shape=[jax.ShapeDtypeStruct((T, H * HEAD_SLOT), _bf16),
                   jax.ShapeDtypeStruct((T, H * HEAD_SLOT), _bf16),
                   jax.ShapeDtypeStruct((B, H * V_HEAD_DIM, S), _bf16),
                   jax.ShapeDtypeStruct((T, D), _bf16),
                   jax.ShapeDtypeStruct((T, D), _bf16)],
        scratch_shapes=[pltpu.VMEM((HALO + TM_IN, C), _f32), pltpu.VMEM((TM_IN, C), _f32),
                        pltpu.VMEM((TM_IN, D), _bf16)],
        compiler_params=pltpu.CompilerParams(dimension_semantics=("arbitrary",),
                                             vmem_limit_bytes=VMEM_LIMIT),
        name="inproj",
    )(x2, pos2, invf4, row(norm_mix_pre[0]), w_small, w_big, row(q_norm[0]), w_uq_k, row(kv_norm[0]),
      w_uk_k, w_uvt_k, cw, row(conv_b[0]), row(conv_ln_g[0]), row(conv_ln_b[0]),
      w_pw2[0].astype(_bf16), row(b_pw2[0]))

    n_blocks, tables = _attn_stream(S)
    assert n_blocks % (2 * ATT_UNROLL) == 0
    n_iter = n_blocks // ATT_UNROLL + 2
    mask_tiles = _attn_mask_tiles()
    qk_spec = pl.BlockSpec((S, ATT_HEADS * HEAD_SLOT), lambda b, g, *_: (b, g))
    attn = pl.pallas_call(
        functools.partial(_attn_kernel, n_iter=n_iter),
        grid_spec=pltpu.PrefetchScalarGridSpec(
            num_scalar_prefetch=len(tables),
            grid=(B, H // ATT_HEADS),
            in_specs=[qk_spec, qk_spec,
                      pl.BlockSpec((None, ATT_HEADS * V_HEAD_DIM, S), lambda b, g, *_: (b, g, 0)),
                      pl.BlockSpec(mask_tiles.shape, lambda b, g, *_: (0, 0, 0))],
            out_specs=pl.BlockSpec((S, ATT_HEADS * V_HEAD_DIM), lambda b, g, *_: (b, g)),
            scratch_shapes=[pltpu.VMEM((ATT_HEADS, 2 * ATT_UNROLL, TK, TQ), _f32),
                            pltpu.VMEM((ATT_HEADS, 2 * ATT_UNROLL, TK, TQ), _bf16)]),
        out_shape=jax.ShapeDtypeStruct((T, H * V_HEAD_DIM), _bf16),
        compiler_params=pltpu.CompilerParams(dimension_semantics=("arbitrary", "arbitrary"),
                                             vmem_limit_bytes=VMEM_LIMIT),
        name="attn",
    )(*[jnp.asarray(t) for t in tables], q, k, vt, jnp.asarray(mask_tiles))

    out = pl.pallas_call(
        _post_kernel,
        grid=(T // TM_POST,),
        in_specs=[_rows(TM_POST, H * V_HEAD_DIM), _rows(TM_POST, D), _rows(TM_POST, D),
                  _rows(TM_POST, D), _resident((H * V_HEAD_DIM, D)), _resident((D, D)),
                  _resident((1, D)), _resident((1, D)), _resident((D, D_FF)),
                  _resident((D_FF, D)), _resident((1, D))],
        out_specs=_rows(TM_POST, D),
        out_shape=jax.ShapeDtypeStruct((T, D), _f32),
        compiler_params=pltpu.CompilerParams(dimension_semantics=("arbitrary",),
                                             vmem_limit_bytes=VMEM_LIMIT),
        name="post",
    )(attn, sg, gc, x2, w_o_attn[0].astype(_bf16), w_out[0].astype(_bf16),
      row(norm_mix_post[0]), row(norm_mlp_pre[0]), w_ff1[0].astype(_bf16),
      w_ff2[0].astype(_bf16), row(norm_mlp_post[0]))
    return out.reshape(B, S, D)
```

```python
import functools
import math

import jax
import jax.numpy as jnp
import numpy as np
from jax import lax
from jax.experimental import pallas as pl
from jax.experimental.pallas import tpu as pltpu

D_MODEL = 1024
N_HEADS = 8
QK_NOPE_DIM = 128
QK_ROPE_DIM = 64
V_HEAD_DIM = 128
Q_LORA_RANK = 384
KV_LORA_RANK = 256
ROPE_THETA = 10000.0
CONV_CHANNELS = 1024
CONV_WIDTH = 31
D_FF = 4096
EPS = 1e-6

HALF_ROPE = QK_ROPE_DIM // 2
ROPE_PACK = 4
ROPE_LANES = ROPE_PACK * HALF_ROPE
HEAD_SLOT = QK_NOPE_DIM + ROPE_LANES
SMALL_COLS = Q_LORA_RANK + KV_LORA_RANK + ROPE_LANES
HALO = 32
SUBLANES = 8
LANES = 128
CONV_ROWS = 64

Q_SCALE = (QK_NOPE_DIM + QK_ROPE_DIM) ** -0.5 * math.log2(math.e)
MASK_VALUE = -1e30

TM_IN = 512
TM_POST = 512
TQ = 256
TK = 512
ATT_HEADS = 2
ATT_UNROLL = 2
SUM_ROWS = 16
FF_CHUNK = 1024

VMEM_LIMIT = 60 * 1024 * 1024

_bf16 = jnp.bfloat16
_f32 = jnp.float32
_NT = (((1,), (1,)), ((), ()))


def _dot(a, b):
    return jnp.dot(a, b, preferred_element_type=_f32)


def _rms(x, g):
    return x * lax.rsqrt(jnp.mean(x * x, axis=-1, keepdims=True) + EPS) * g


def _inproj_kernel(x_ref, pos_ref, invf_ref, g_pre_ref, w_small_ref, w_big_ref, qn_g_ref, w_uq_ref,
                   kvn_g_ref, w_uk_ref, w_uvt_ref, cw_ref, cb_ref, lng_ref, lnb_ref,
                   w_pw2_ref, b_pw2_ref,
                   q_ref, k_ref, vt_ref, sg_ref, gc_ref,
                   ubuf, cacc, hbuf, *, tiles_per_seq):
    tm = x_ref.shape[0]
    C = CONV_CHANNELS

    @pl.when(pl.program_id(0) % tiles_per_seq == 0)
    def _():
        ubuf[0:HALO, :] = jnp.zeros((HALO, C), _f32)

    hbuf[...] = _rms(x_ref[...], g_pre_ref[...]).astype(_bf16)
    a = _dot(hbuf[...], w_big_ref[:, 0:C])
    b = _dot(hbuf[...], w_big_ref[:, C:2 * C])
    ubuf[HALO:HALO + tm, :] = a * jax.nn.sigmoid(b)

    shift = HALO - (CONV_WIDTH - 1)
    win = CONV_ROWS + HALO

    def conv_lanes(c, carry):
        l0 = pl.multiple_of(c * LANES, LANES)
        for r0 in range(0, tm, CONV_ROWS):
            window = ubuf[r0:r0 + win, pl.ds(l0, LANES)]
            acc = jnp.broadcast_to(cb_ref[:, pl.ds(l0, LANES)], (CONV_ROWS, LANES))
            for phase in range(SUBLANES):
                rolled = window if phase == 0 else pltpu.roll(window, win - phase, 0)
                for a in range(win // SUBLANES):
                    t = SUBLANES * a + phase - shift
                    if 0 <= t < CONV_WIDTH:
                        acc = acc + (rolled[SUBLANES * a:SUBLANES * a + CONV_ROWS, :]
                                     * cw_ref[pl.ds(t, 1), pl.ds(l0, LANES)])
            cacc[r0:r0 + CONV_ROWS, pl.ds(l0, LANES)] = acc
        return carry

    lax.fori_loop(0, C // LANES, conv_lanes, 0)
    ubuf[0:HALO, :] = ubuf[tm:tm + HALO, :]

    conv = cacc[...]
    xc = conv - jnp.mean(conv, axis=-1, keepdims=True)
    var = jnp.mean(xc * xc, axis=-1, keepdims=True)
    y = xc * lax.rsqrt(var + EPS) * lng_ref[...] + lnb_ref[...]
    swish = (y * jax.nn.sigmoid(y)).astype(_bf16)

    h = hbuf[...]
    sg_ref[...] = jax.nn.sigmoid(_dot(h, w_big_ref[:, 2 * C:3 * C])).astype(_bf16)
    g_conv = jax.nn.sigmoid(_dot(h, w_big_ref[:, 3 * C:4 * C]))

    zs = _dot(h, w_small_ref[...])
    cq = zs[:, 0:Q_LORA_RANK]
    ckv = zs[:, Q_LORA_RANK:Q_LORA_RANK + KV_LORA_RANK]
    kr4 = zs[:, Q_LORA_RANK + KV_LORA_RANK:SMALL_COLS]

    pos = pos_ref[...].astype(_f32)
    lane = lax.broadcasted_iota(jnp.int32, (tm // ROPE_PACK, ROPE_LANES), 1)
    in_group = [lane < (g + 1) * HALF_ROPE for g in range(ROPE_PACK - 1)]

    def by_group(parts):
        out = parts[-1]
        for g in reversed(range(ROPE_PACK - 1)):
            out = jnp.where(in_group[g], parts[g], out)
        return out

    ang = by_group([pos[:, g:g + 1] for g in range(ROPE_PACK)]) * invf_ref[...]
    cos, sin = jnp.cos(ang), jnp.sin(ang)
    cos_rot = [cos] + [pltpu.roll(cos, g * HALF_ROPE, 1) for g in range(1, ROPE_PACK)]
    sin_rot = [sin] + [pltpu.roll(sin, g * HALF_ROPE, 1) for g in range(1, ROPE_PACK)]
    rope4 = jnp.concatenate(
        [by_group([cos_rot[(0 - g) % ROPE_PACK], cos_rot[(1 - g) % ROPE_PACK],
                   -sin_rot[(2 - g) % ROPE_PACK], sin_rot[(3 - g) % ROPE_PACK]])
         for g in range(ROPE_PACK)], axis=0)

    q = _dot(_rms(cq, qn_g_ref[...]).astype(_bf16), w_uq_ref[...])
    rope4_q = rope4 * Q_SCALE
    for hd in range(N_HEADS):
        lo = hd * HEAD_SLOT
        q_ref[:, lo:lo + QK_NOPE_DIM] = (q[:, lo:lo + QK_NOPE_DIM] * Q_SCALE).astype(_bf16)
        q_ref[:, lo + QK_NOPE_DIM:lo + HEAD_SLOT] = (
            q[:, lo + QK_NOPE_DIM:lo + HEAD_SLOT] * rope4_q).astype(_bf16)

    kp = kr4 * rope4
    kdup = (kp + pltpu.roll(kp, 2 * HALF_ROPE, 1)).astype(_bf16)
    ckvn = _rms(ckv, kvn_g_ref[...]).astype(_bf16)
    kn = _dot(ckvn, w_uk_ref[...])
    for hd in range(N_HEADS):
        lo = hd * HEAD_SLOT
        k_ref[:, lo:lo + QK_NOPE_DIM] = kn[:, hd * QK_NOPE_DIM:(hd + 1) * QK_NOPE_DIM].astype(_bf16)
        k_ref[:, lo + QK_NOPE_DIM:lo + HEAD_SLOT] = kdup
    vt_ref[...] = lax.dot_general(w_uvt_ref[...], ckvn, _NT,
                                  preferred_element_type=_f32).astype(_bf16)

    y_conv = _dot(swish, w_pw2_ref[...]) + b_pw2_ref[...]
    gc_ref[...] = (g_conv * y_conv).astype(_bf16)


def _attn_stream(seq):
    qi, kj, first, last, kind = [], [], [], [], []
    for i in range(seq // TQ):
        n_blocks = (i * TQ + TQ - 1) // TK + 1
        for j in range(n_blocks):
            qi.append(i)
            kj.append(j)
            first.append(int(j == 0))
            last.append(int(j == n_blocks - 1))
            kind.append(0 if j < n_blocks - 1 else 1 + (i * TQ - j * TK) // TQ)
    pad = [0] * (2 * ATT_UNROLL)
    tables = [np.asarray(pad + t + pad, np.int32) for t in (qi, kj, first, last, kind)]
    return len(qi), tables


def _attn_mask_tiles():
    kk = np.arange(TK)[:, None]
    qq = np.arange(TQ)[None, :]
    tiles = [np.zeros((TK, TQ), np.float32)]
    for d in range(TK // TQ):
        tiles.append(np.where(kk <= qq + d * TQ, 0.0, MASK_VALUE).astype(np.float32))
    return np.stack(tiles)


def _attn_kernel(qi_ref, kj_ref, first_ref, last_ref, kind_ref,
                 q_ref, k_ref, vt_ref, mask_ref, o_ref, s_buf, p_buf, *, n_iter):
    U = ATT_UNROLL
    s_buf[...] = jnp.zeros(s_buf.shape, _f32)
    p_buf[...] = jnp.zeros(p_buf.shape, _bf16)

    def head_cols(ref, hd, width, rows):
        return ref[rows, hd * width:(hd + 1) * width]

    def stage_a(hd, e, slot):
        q0 = pl.multiple_of(qi_ref[e] * TQ, TQ)
        k0 = pl.multiple_of(kj_ref[e] * TK, TK)
        s = lax.dot_general(head_cols(k_ref, hd, HEAD_SLOT, pl.ds(k0, TK)),
                            head_cols(q_ref, hd, HEAD_SLOT, pl.ds(q0, TQ)), _NT,
                            preferred_element_type=_f32)
        s = s + mask_ref[kind_ref[e]]
        s_buf[hd, slot] = s
        return jnp.max(s, axis=0, keepdims=True)

    def stage_b(hd, e, slot, m, cmax):
        m_prev = jnp.where(first_ref[e] != 0, -jnp.inf, m)
        m_new = jnp.maximum(m_prev, cmax)
        p_buf[hd, slot] = jnp.exp2(s_buf[hd, slot] - m_new).astype(_bf16)
        return m_new, jnp.exp2(m_prev - m_new)

    ones_rows = jnp.ones((SUM_ROWS, TK), _bf16)

    def stage_c(hd, e, slot, acc, l, alpha):
        k0 = pl.multiple_of(kj_ref[e] * TK, TK)
        vt = vt_ref[hd * V_HEAD_DIM:(hd + 1) * V_HEAD_DIM, pl.ds(k0, TK)]
        pv = _dot(jnp.concatenate([vt, ones_rows], axis=0), p_buf[hd, slot])
        return alpha * acc + pv[:V_HEAD_DIM], alpha * l + pv[V_HEAD_DIM:V_HEAD_DIM + 1]

    def iteration(it, parity, carry):
        state = [dict(c) for c in carry]
        alpha_new = [list(st["alpha"]) for st in state]
        for u in range(U):
            for hd in range(ATT_HEADS):
                st = state[hd]
                st["m"], alpha_new[hd][u] = stage_b(hd, (it + 1) * U + u, (1 - parity) * U + u,
                                                    st["m"], st["cmax"][u])
        cmax_new = [[None] * U for _ in range(ATT_HEADS)]
        for u in range(U):
            for hd in range(ATT_HEADS):
                cmax_new[hd][u] = stage_a(hd, (it + 2) * U + u, parity * U + u)
        done = []
        for u in range(U):
            for hd in range(ATT_HEADS):
                st = state[hd]
                e = it * U + u
                st["acc"], st["l"] = stage_c(hd, e, parity * U + u, st["acc"], st["l"],
                                             st["alpha"][u])
                done.append((hd, e, st["acc"], st["l"]))
        for hd in range(ATT_HEADS):
            state[hd]["alpha"] = tuple(alpha_new[hd])
            state[hd]["cmax"] = tuple(cmax_new[hd])
        for hd, e, acc, l in done:
            @pl.when(last_ref[e] != 0)
            def _():
                q0 = pl.multiple_of(qi_ref[e] * TQ, TQ)
                o_ref[pl.ds(q0, TQ), hd * V_HEAD_DIM:(hd + 1) * V_HEAD_DIM] = (
                    (acc / l).T.astype(_bf16))
        return tuple(state)

    def two_iterations(i, carry):
        return iteration(2 * i + 1, 1, iteration(2 * i, 0, carry))

    zeros = jnp.zeros((1, TQ), _f32)
    init = dict(m=zeros, l=zeros, acc=jnp.zeros((V_HEAD_DIM, TQ), _f32),
                alpha=(zeros,) * U, cmax=(zeros,) * U)
    lax.fori_loop(0, n_iter // 2, two_iterations, (init,) * ATT_HEADS)


def _post_kernel(attn_ref, sg_ref, gc_ref, x_ref, w_o_ref, w_out_ref, g_mix_post_ref,
                 g_mlp_pre_ref, w_ff1_ref, w_ff2_ref, g_mlp_post_ref, out_ref):
    y_attn = _dot(attn_ref[...], w_o_ref[...])
    merged = sg_ref[...].astype(_f32) * y_attn + gc_ref[...].astype(_f32)
    m = _dot(merged.astype(_bf16), w_out_ref[...])
    x1 = x_ref[...] + _rms(m, g_mix_post_ref[...])
    h2 = _rms(x1, g_mlp_pre_ref[...]).astype(_bf16)
    f = jnp.zeros(x1.shape, _f32)
    for c in range(D_FF // FF_CHUNK):
        hc = jnp.maximum(_dot(h2, w_ff1_ref[:, c * FF_CHUNK:(c + 1) * FF_CHUNK]), 0.0)
        f = f + _dot((hc * hc).astype(_bf16), w_ff2_ref[c * FF_CHUNK:(c + 1) * FF_CHUNK, :])
    out_ref[...] = x1 + _rms(f, g_mlp_post_ref[...])


def _resident(shape):
    return pl.BlockSpec(shape, lambda *_: (0,) * len(shape), pipeline_mode=pl.Buffered(1))


def _rows(tm, cols):
    return pl.BlockSpec((tm, cols), lambda i: (i, 0))


def kernel(x, positions, norm_mix_pre, w_in, q_norm, w_uq, kv_norm, w_uk, w_uv, w_o_attn,
           conv_w, conv_b, conv_ln_g, conv_ln_b, w_pw2, b_pw2, w_out, norm_mix_post,
           norm_mlp_pre, w_ff1, w_ff2, norm_mlp_post):
    B, S, D = x.shape
    T = B * S
    assert D == D_MODEL and w_in.shape[0] == 1
    assert S % TM_IN == 0 and S % TK == 0 and TK % TQ == 0 and T % TM_POST == 0
    assert N_HEADS % ATT_HEADS == 0
    C = CONV_CHANNELS
    H = N_HEADS
    tiles_per_seq = S // TM_IN

    w_in0 = w_in[0]
    o_kv, o_kr = Q_LORA_RANK + KV_LORA_RANK, Q_LORA_RANK + KV_LORA_RANK + QK_ROPE_DIM
    k1 = w_in0[:, o_kv:o_kv + HALF_ROPE]
    k2 = w_in0[:, o_kv + HALF_ROPE:o_kr]
    w_small = jnp.concatenate([w_in0[:, :o_kv], k1, k2, k2, k1], axis=1).astype(_bf16)
    w_big = w_in0[:, o_kr:].astype(_bf16)

    wq = w_uq[0].reshape(Q_LORA_RANK, H, QK_NOPE_DIM + QK_ROPE_DIM)
    r1 = wq[:, :, QK_NOPE_DIM:QK_NOPE_DIM + HALF_ROPE]
    r2 = wq[:, :, QK_NOPE_DIM + HALF_ROPE:]
    w_uq_k = jnp.concatenate([wq[:, :, :QK_NOPE_DIM], r1, r2, r2, r1], axis=2)
    w_uq_k = w_uq_k.reshape(Q_LORA_RANK, H * HEAD_SLOT).astype(_bf16)
    w_uk_k = w_uk[0].reshape(KV_LORA_RANK, H * QK_NOPE_DIM).astype(_bf16)
    w_uvt_k = w_uv[0].reshape(KV_LORA_RANK, H * V_HEAD_DIM).T.astype(_bf16)
    cw = jnp.pad(conv_w[0].reshape(CONV_WIDTH, C), ((0, 1), (0, 0)))

    inv_freq = ROPE_THETA ** (-jnp.arange(0, QK_ROPE_DIM, 2, dtype=_f32) / QK_ROPE_DIM)
    invf4 = jnp.tile(inv_freq, 4).reshape(1, ROPE_LANES)

    row = lambda a: a.reshape(1, -1)
    x2 = x.reshape(T, D)
    pos2 = positions.reshape(T // TM_IN, ROPE_PACK, TM_IN // ROPE_PACK).transpose(0, 2, 1)

    q, k, vt, sg, gc = pl.pallas_call(
        functools.partial(_inproj_kernel, tiles_per_seq=tiles_per_seq),
        grid=(T // TM_IN,),
        in_specs=[
            _rows(TM_IN, D),
            pl.BlockSpec((None, TM_IN // ROPE_PACK, ROPE_PACK), lambda i: (i, 0, 0)),
            _resident((1, ROPE_LANES)), _resident((1, D)),
            _resident(w_small.shape), _resident(w_big.shape),
            _resident((1, Q_LORA_RANK)), _resident(w_uq_k.shape),
            _resident((1, KV_LORA_RANK)), _resident(w_uk_k.shape), _resident(w_uvt_k.shape),
            _resident(cw.shape), _resident((1, C)), _resident((1, C)), _resident((1, C)),
            _resident((C, D)), _resident((1, D)),
        ],
        out_specs=[_rows(TM_IN, H * HEAD_SLOT), _rows(TM_IN, H * HEAD_SLOT),
                   pl.BlockSpec((None, H * V_HEAD_DIM, TM_IN),
                                lambda i: (i // tiles_per_seq, 0, i % tiles_per_seq)),
                   _rows(TM_IN, D), _rows(TM_IN, D)],
        out_shape=[jax.ShapeDtypeStruct((T, H * HEAD_SLOT), _bf16),
                   jax.ShapeDtypeStruct((T, H * HEAD_SLOT), _bf16),
                   jax.ShapeDtypeStruct((B, H * V_HEAD_DIM, S), _bf16),
                   jax.ShapeDtypeStruct((T, D), _bf16),
                   jax.ShapeDtypeStruct((T, D), _bf16)],
        scratch_shapes=[pltpu.VMEM((HALO + TM_IN, C), _f32), pltpu.VMEM((TM_IN, C), _f32),
                        pltpu.VMEM((TM_IN, D), _bf16)],
        compiler_params=pltpu.CompilerParams(dimension_semantics=("arbitrary",),
                                             vmem_limit_bytes=VMEM_LIMIT),
        name="inproj",
    )(x2, pos2, invf4, row(norm_mix_pre[0]), w_small, w_big, row(q_norm[0]), w_uq_k, row(kv_norm[0]),
      w_uk_k, w_uvt_k, cw, row(conv_b[0]), row(conv_ln_g[0]), row(conv_ln_b[0]),
      w_pw2[0].astype(_bf16), row(b_pw2[0]))

    n_blocks, tables = _attn_stream(S)
    assert n_blocks % (2 * ATT_UNROLL) == 0
    n_iter = n_blocks // ATT_UNROLL + 2
    mask_tiles = _attn_mask_tiles()
    qk_spec = pl.BlockSpec((S, ATT_HEADS * HEAD_SLOT), lambda b, g, *_: (b, g))
    attn = pl.pallas_call(
        functools.partial(_attn_kernel, n_iter=n_iter),
        grid_spec=pltpu.PrefetchScalarGridSpec(
            num_scalar_prefetch=len(tables),
            grid=(B, H // ATT_HEADS),
            in_specs=[qk_spec, qk_spec,
                      pl.BlockSpec((None, ATT_HEADS * V_HEAD_DIM, S), lambda b, g, *_: (b, g, 0)),
                      pl.BlockSpec(mask_tiles.shape, lambda b, g, *_: (0, 0, 0))],
            out_specs=pl.BlockSpec((S, ATT_HEADS * V_HEAD_DIM), lambda b, g, *_: (b, g)),
            scratch_shapes=[pltpu.VMEM((ATT_HEADS, 2 * ATT_UNROLL, TK, TQ), _f32),
                            pltpu.VMEM((ATT_HEADS, 2 * ATT_UNROLL, TK, TQ), _bf16)]),
        out_shape=jax.ShapeDtypeStruct((T, H * V_HEAD_DIM), _bf16),
        compiler_params=pltpu.CompilerParams(dimension_semantics=("arbitrary", "arbitrary"),
                                             vmem_limit_bytes=VMEM_LIMIT),
        name="attn",
    )(*[jnp.asarray(t) for t in tables], q, k, vt, jnp.asarray(mask_tiles))

    out = pl.pallas_call(
        _post_kernel,
        grid=(T // TM_POST,),
        in_specs=[_rows(TM_POST, H * V_HEAD_DIM), _rows(TM_POST, D), _rows(TM_POST, D),
                  _rows(TM_POST, D), _resident((H * V_HEAD_DIM, D)), _resident((D, D)),
                  _resident((1, D)), _resident((1, D)), _resident((D, D_FF)),
                  _resident((D_FF, D)), _resident((1, D))],
        out_specs=_rows(TM_POST, D),
        out_shape=jax.ShapeDtypeStruct((T, D), _f32),
        compiler_params=pltpu.CompilerParams(dimension_semantics=("arbitrary",),
                                             vmem_limit_bytes=VMEM_LIMIT),
        name="post",
    )(attn, sg, gc, x2, w_o_attn[0].astype(_bf16), w_out[0].astype(_bf16),
      row(norm_mix_post[0]), row(norm_mlp_pre[0]), w_ff1[0].astype(_bf16),
      w_ff2[0].astype(_bf16), row(norm_mlp_post[0]))
    return out.reshape(B, S, D)
```

```python
import functools
import math

import jax
import jax.numpy as jnp
import numpy as np
from jax import lax
from jax.experimental import pallas as pl
from jax.experimental.pallas import tpu as pltpu

D_MODEL = 1024
N_HEADS = 8
QK_NOPE_DIM = 128
QK_ROPE_DIM = 64
V_HEAD_DIM = 128
Q_LORA_RANK = 384
KV_LORA_RANK = 256
ROPE_THETA = 10000.0
CONV_CHANNELS = 1024
CONV_WIDTH = 31
D_FF = 4096
EPS = 1e-6

HALF_ROPE = QK_ROPE_DIM // 2
ROPE_PACK = 4
ROPE_LANES = ROPE_PACK * HALF_ROPE
HEAD_SLOT = QK_NOPE_DIM + ROPE_LANES
SMALL_COLS = Q_LORA_RANK + KV_LORA_RANK + ROPE_LANES
HALO = 32
SUBLANES = 8
LANES = 128
CONV_ROWS = 64

Q_SCALE = (QK_NOPE_DIM + QK_ROPE_DIM) ** -0.5 * math.log2(math.e)
MASK_VALUE = -1e30

TM_IN = 512
TM_POST = 512
TQ = 256
TK = 512
ATT_HEADS = 2
ATT_UNROLL = 2
SUM_ROWS = 16
FF_CHUNK = 1024

VMEM_LIMIT = 60 * 1024 * 1024

_bf16 = jnp.bfloat16
_f32 = jnp.float32
_NT = (((1,), (1,)), ((), ()))


def _dot(a, b):
    return jnp.dot(a, b, preferred_element_type=_f32)


def _rms(x, g):
    return x * lax.rsqrt(jnp.mean(x * x, axis=-1, keepdims=True) + EPS) * g


def _inproj_kernel(x_ref, pos_ref, invf_ref, g_pre_ref, w_small_ref, w_big_ref, qn_g_ref, w_uq_ref,
                   kvn_g_ref, w_uk_ref, w_uvt_ref, cw_ref, cb_ref, lng_ref, lnb_ref,
                   w_pw2_ref, b_pw2_ref,
                   q_ref, k_ref, vt_ref, sg_ref, gc_ref,
                   ubuf, cacc, hbuf, *, tiles_per_seq):
    tm = x_ref.shape[0]
    C = CONV_CHANNELS

    @pl.when(pl.program_id(0) % tiles_per_seq == 0)
    def _():
        ubuf[0:HALO, :] = jnp.zeros((HALO, C), _f32)

    hbuf[...] = _rms(x_ref[...], g_pre_ref[...]).astype(_bf16)
    a = _dot(hbuf[...], w_big_ref[:, 0:C])
    b = _dot(hbuf[...], w_big_ref[:, C:2 * C])
    ubuf[HALO:HALO + tm, :] = a * jax.nn.sigmoid(b)

    shift = HALO - (CONV_WIDTH - 1)
    win = CONV_ROWS + HALO

    def conv_lanes(c, carry):
        l0 = pl.multiple_of(c * LANES, LANES)
        for r0 in range(0, tm, CONV_ROWS):
            window = ubuf[r0:r0 + win, pl.ds(l0, LANES)]
            acc = jnp.broadcast_to(cb_ref[:, pl.ds(l0, LANES)], (CONV_ROWS, LANES))
            for phase in range(SUBLANES):
                rolled = window if phase == 0 else pltpu.roll(window, win - phase, 0)
                for a in range(win // SUBLANES):
                    t = SUBLANES * a + phase - shift
                    if 0 <= t < CONV_WIDTH:
                        acc = acc + (rolled[SUBLANES * a:SUBLANES * a + CONV_ROWS, :]
                                     * cw_ref[pl.ds(t, 1), pl.ds(l0, LANES)])
            cacc[r0:r0 + CONV_ROWS, pl.ds(l0, LANES)] = acc
        return carry

    lax.fori_loop(0, C // LANES, conv_lanes, 0)
    ubuf[0:HALO, :] = ubuf[tm:tm + HALO, :]

    conv = cacc[...]
    xc = conv - jnp.mean(conv, axis=-1, keepdims=True)
    var = jnp.mean(xc * xc, axis=-1, keepdims=True)
    y = xc * lax.rsqrt(var + EPS) * lng_ref[...] + lnb_ref[...]
    swish = (y * jax.nn.sigmoid(y)).astype(_bf16)

    h = hbuf[...]
    sg_ref[...] = jax.nn.sigmoid(_dot(h, w_big_ref[:, 2 * C:3 * C])).astype(_bf16)
    g_conv = jax.nn.sigmoid(_dot(h, w_big_ref[:, 3 * C:4 * C]))

    zs = _dot(h, w_small_ref[...])
    cq = zs[:, 0:Q_LORA_RANK]
    ckv = zs[:, Q_LORA_RANK:Q_LORA_RANK + KV_LORA_RANK]
    kr4 = zs[:, Q_LORA_RANK + KV_LORA_RANK:SMALL_COLS]

    pos = pos_ref[...].astype(_f32)
    lane = lax.broadcasted_iota(jnp.int32, (tm // ROPE_PACK, ROPE_LANES), 1)
    in_group = [lane < (g + 1) * HALF_ROPE for g in range(ROPE_PACK - 1)]

    def by_group(parts):
        out = parts[-1]
        for g in reversed(range(ROPE_PACK - 1)):
            out = jnp.where(in_group[g], parts[g], out)
        return out

    ang = by_group([pos[:, g:g + 1] for g in range(ROPE_PACK)]) * invf_ref[...]
    cos, sin = jnp.cos(ang), jnp.sin(ang)
    cos_rot = [cos] + [pltpu.roll(cos, g * HALF_ROPE, 1) for g in range(1, ROPE_PACK)]
    sin_rot = [sin] + [pltpu.roll(sin, g * HALF_ROPE, 1) for g in range(1, ROPE_PACK)]
    rope4 = jnp.concatenate(
        [by_group([cos_rot[(0 - g) % ROPE_PACK], cos_rot[(1 - g) % ROPE_PACK],
                   -sin_rot[(2 - g) % ROPE_PACK], sin_rot[(3 - g) % ROPE_PACK]])
         for g in range(ROPE_PACK)], axis=0)

    q = _dot(_rms(cq, qn_g_ref[...]).astype(_bf16), w_uq_ref[...])
    rope4_q = rope4 * Q_SCALE
    for hd in range(N_HEADS):
        lo = hd * HEAD_SLOT
        q_ref[:, lo:lo + QK_NOPE_DIM] = (q[:, lo:lo + QK_NOPE_DIM] * Q_SCALE).astype(_bf16)
        q_ref[:, lo + QK_NOPE_DIM:lo + HEAD_SLOT] = (
            q[:, lo + QK_NOPE_DIM:lo + HEAD_SLOT] * rope4_q).astype(_bf16)

    kp = kr4 * rope4
    kdup = (kp + pltpu.roll(kp, 2 * HALF_ROPE, 1)).astype(_bf16)
    ckvn = _rms(ckv, kvn_g_ref[...]).astype(_bf16)
    kn = _dot(ckvn, w_uk_ref[...])
    for hd in range(N_HEADS):
        lo = hd * HEAD_SLOT
        k_ref[:, lo:lo + QK_NOPE_DIM] = kn[:, hd * QK_NOPE_DIM:(hd + 1) * QK_NOPE_DIM].astype(_bf16)
        k_ref[:, lo + QK_NOPE_DIM:lo + HEAD_SLOT] = kdup
    vt_ref[...] = lax.dot_general(w_uvt_ref[...], ckvn, _NT,
                                  preferred_element_type=_f32).astype(_bf16)

    y_conv = _dot(swish, w_pw2_ref[...]) + b_pw2_ref[...]
    gc_ref[...] = (g_conv * y_conv).astype(_bf16)


def _attn_stream(seq):
    n_tiles = seq // TQ
    qi, kj, first, kind, out = [], [], [], [], []
    for i in range(n_tiles):
        n_blocks = (i * TQ + TQ - 1) // TK + 1
        for j in range(n_blocks):
            qi.append(i)
            kj.append(j)
            first.append(int(j == 0))
            kind.append(0 if j < n_blocks - 1 else 1 + (i * TQ - j * TK) // TQ)
            out.append(i)
    n_pad = 2 * ATT_UNROLL
    tables = [np.asarray([0] * n_pad + t + [0] * n_pad, np.int32) for t in (qi, kj, first, kind)]
    tables.append(np.asarray([n_tiles] * n_pad + out + [n_tiles] * n_pad, np.int32))
    return len(qi), tables


def _attn_mask_tiles():
    kk = np.arange(TK)[:, None]
    qq = np.arange(TQ)[None, :]
    tiles = [np.zeros((TK, TQ), np.float32)]
    for d in range(TK // TQ):
        tiles.append(np.where(kk <= qq + d * TQ, 0.0, MASK_VALUE).astype(np.float32))
    return np.stack(tiles)


def _attn_kernel(qi_ref, kj_ref, first_ref, kind_ref, out_ref,
                 q_ref, k_ref, vt_ref, mask_ref, o_ref, s_buf, p_buf, acc_buf, l_buf, *, n_iter):
    U = ATT_UNROLL
    n_tiles = q_ref.shape[0] // TQ
    s_buf[...] = jnp.zeros(s_buf.shape, _f32)
    p_buf[...] = jnp.zeros(p_buf.shape, _bf16)
    acc_buf[...] = jnp.zeros(acc_buf.shape, _f32)
    l_buf[...] = jnp.zeros(l_buf.shape, _f32)

    def head_cols(ref, hd, width, rows):
        return ref[rows, hd * width:(hd + 1) * width]

    def stage_a(hd, e, slot):
        q0 = pl.multiple_of(qi_ref[e] * TQ, TQ)
        k0 = pl.multiple_of(kj_ref[e] * TK, TK)
        s = lax.dot_general(head_cols(k_ref, hd, HEAD_SLOT, pl.ds(k0, TK)),
                            head_cols(q_ref, hd, HEAD_SLOT, pl.ds(q0, TQ)), _NT,
                            preferred_element_type=_f32)
        s = s + mask_ref[kind_ref[e]]
        s_buf[hd, slot] = s
        return jnp.max(s, axis=0, keepdims=True)

    def stage_b(hd, e, slot, m, cmax):
        m_prev = jnp.where(first_ref[e] != 0, -jnp.inf, m)
        m_new = jnp.maximum(m_prev, cmax)
        p_buf[hd, slot] = jnp.exp2(s_buf[hd, slot] - m_new).astype(_bf16)
        return m_new, jnp.exp2(m_prev - m_new)

    ones_rows = jnp.ones((SUM_ROWS, TK), _bf16)

    def stage_c(hd, e, slot, alpha):
        k0 = pl.multiple_of(kj_ref[e] * TK, TK)
        tile = out_ref[e]
        vt = vt_ref[hd * V_HEAD_DIM:(hd + 1) * V_HEAD_DIM, pl.ds(k0, TK)]
        pv = _dot(jnp.concatenate([vt, ones_rows], axis=0), p_buf[hd, slot])
        acc_buf[hd, tile] = alpha * acc_buf[hd, tile] + pv[:V_HEAD_DIM]
        l_buf[hd, tile] = alpha * l_buf[hd, tile] + pv[V_HEAD_DIM:V_HEAD_DIM + 1]

    def iteration(it, parity, carry):
        state = [dict(c) for c in carry]
        alpha_new = [list(st["alpha"]) for st in state]
        for u in range(U):
            for hd in range(ATT_HEADS):
                st = state[hd]
                st["m"], alpha_new[hd][u] = stage_b(hd, (it + 1) * U + u, (1 - parity) * U + u,
                                                    st["m"], st["cmax"][u])
        cmax_new = [[None] * U for _ in range(ATT_HEADS)]
        for u in range(U):
            for hd in range(ATT_HEADS):
                cmax_new[hd][u] = stage_a(hd, (it + 2) * U + u, parity * U + u)
        for u in range(U):
            for hd in range(ATT_HEADS):
                stage_c(hd, it * U + u, parity * U + u, state[hd]["alpha"][u])
        for hd in range(ATT_HEADS):
            state[hd]["alpha"] = tuple(alpha_new[hd])
            state[hd]["cmax"] = tuple(cmax_new[hd])
        return tuple(state)

    def two_iterations(i, carry):
        return iteration(2 * i + 1, 1, iteration(2 * i, 0, carry))

    zeros = jnp.zeros((1, TQ), _f32)
    init = dict(m=zeros, alpha=(zeros,) * U, cmax=(zeros,) * U)
    lax.fori_loop(0, n_iter // 2, two_iterations, (init,) * ATT_HEADS)

    def write_tile(t, carry):
        q0 = pl.multiple_of(t * TQ, TQ)
        for hd in range(ATT_HEADS):
            o_ref[pl.ds(q0, TQ), hd * V_HEAD_DIM:(hd + 1) * V_HEAD_DIM] = (
                (acc_buf[hd, t] / l_buf[hd, t]).T.astype(_bf16))
        return carry

    lax.fori_loop(0, n_tiles, write_tile, 0)


def _post_kernel(attn_ref, sg_ref, gc_ref, x_ref, w_o_ref, w_out_ref, g_mix_post_ref,
                 g_mlp_pre_ref, w_ff1_ref, w_ff2_ref, g_mlp_post_ref, out_ref):
    y_attn = _dot(attn_ref[...], w_o_ref[...])
    merged = sg_ref[...].astype(_f32) * y_attn + gc_ref[...].astype(_f32)
    m = _dot(merged.astype(_bf16), w_out_ref[...])
    x1 = x_ref[...] + _rms(m, g_mix_post_ref[...])
    h2 = _rms(x1, g_mlp_pre_ref[...]).astype(_bf16)
    f = jnp.zeros(x1.shape, _f32)
    for c in range(D_FF // FF_CHUNK):
        hc = jnp.maximum(_dot(h2, w_ff1_ref[:, c * FF_CHUNK:(c + 1) * FF_CHUNK]), 0.0)
        f = f + _dot((hc * hc).astype(_bf16), w_ff2_ref[c * FF_CHUNK:(c + 1) * FF_CHUNK, :])
    out_ref[...] = x1 + _rms(f, g_mlp_post_ref[...])


def _resident(shape):
    return pl.BlockSpec(shape, lambda *_: (0,) * len(shape), pipeline_mode=pl.Buffered(1))


def _rows(tm, cols):
    return pl.BlockSpec((tm, cols), lambda i: (i, 0))


def kernel(x, positions, norm_mix_pre, w_in, q_norm, w_uq, kv_norm, w_uk, w_uv, w_o_attn,
           conv_w, conv_b, conv_ln_g, conv_ln_b, w_pw2, b_pw2, w_out, norm_mix_post,
           norm_mlp_pre, w_ff1, w_ff2, norm_mlp_post):
    B, S, D = x.shape
    T = B * S
    assert D == D_MODEL and w_in.shape[0] == 1
    assert S % TM_IN == 0 and S % TK == 0 and TK % TQ == 0 and T % TM_POST == 0
    assert N_HEADS % ATT_HEADS == 0
    C = CONV_CHANNELS
    H = N_HEADS
    tiles_per_seq = S // TM_IN

    w_in0 = w_in[0]
    o_kv, o_kr = Q_LORA_RANK + KV_LORA_RANK, Q_LORA_RANK + KV_LORA_RANK + QK_ROPE_DIM
    k1 = w_in0[:, o_kv:o_kv + HALF_ROPE]
    k2 = w_in0[:, o_kv + HALF_ROPE:o_kr]
    w_small = jnp.concatenate([w_in0[:, :o_kv], k1, k2, k2, k1], axis=1).astype(_bf16)
    w_big = w_in0[:, o_kr:].astype(_bf16)

    wq = w_uq[0].reshape(Q_LORA_RANK, H, QK_NOPE_DIM + QK_ROPE_DIM)
    r1 = wq[:, :, QK_NOPE_DIM:QK_NOPE_DIM + HALF_ROPE]
    r2 = wq[:, :, QK_NOPE_DIM + HALF_ROPE:]
    w_uq_k = jnp.concatenate([wq[:, :, :QK_NOPE_DIM], r1, r2, r2, r1], axis=2)
    w_uq_k = w_uq_k.reshape(Q_LORA_RANK, H * HEAD_SLOT).astype(_bf16)
    w_uk_k = w_uk[0].reshape(KV_LORA_RANK, H * QK_NOPE_DIM).astype(_bf16)
    w_uvt_k = w_uv[0].reshape(KV_LORA_RANK, H * V_HEAD_DIM).T.astype(_bf16)
    cw = jnp.pad(conv_w[0].reshape(CONV_WIDTH, C), ((0, 1), (0, 0)))

    inv_freq = ROPE_THETA ** (-jnp.arange(0, QK_ROPE_DIM, 2, dtype=_f32) / QK_ROPE_DIM)
    invf4 = jnp.tile(inv_freq, 4).reshape(1, ROPE_LANES)

    row = lambda a: a.reshape(1, -1)
    x2 = x.reshape(T, D)
    pos2 = positions.reshape(T // TM_IN, ROPE_PACK, TM_IN // ROPE_PACK).transpose(0, 2, 1)

    q, k, vt, sg, gc = pl.pallas_call(
        functools.partial(_inproj_kernel, tiles_per_seq=tiles_per_seq),
        grid=(T // TM_IN,),
        in_specs=[
            _rows(TM_IN, D),
            pl.BlockSpec((None, TM_IN // ROPE_PACK, ROPE_PACK), lambda i: (i, 0, 0)),
            _resident((1, ROPE_LANES)), _resident((1, D)),
            _resident(w_small.shape), _resident(w_big.shape),
            _resident((1, Q_LORA_RANK)), _resident(w_uq_k.shape),
            _resident((1, KV_LORA_RANK)), _resident(w_uk_k.shape), _resident(w_uvt_k.shape),
            _resident(cw.shape), _resident((1, C)), _resident((1, C)), _resident((1, C)),
            _resident((C, D)), _resident((1, D)),
        ],
        out_specs=[_rows(TM_IN, H * HEAD_SLOT), _rows(TM_IN, H * HEAD_SLOT),
                   pl.BlockSpec((None, H * V_HEAD_DIM, TM_IN),
                                lambda i: (i // tiles_per_seq, 0, i % tiles_per_seq)),
                   _rows(TM_IN, D), _rows(TM_IN, D)],
        out_shape=[jax.ShapeDtypeStruct((T, H * HEAD_SLOT), _bf16),
                   jax.ShapeDtypeStruct((T, H * HEAD_SLOT), _bf16),
                   jax.ShapeDtypeStruct((B, H * V_HEAD_DIM, S), _bf16),
                   jax.ShapeDtypeStruct((T, D), _bf16),
                   jax.ShapeDtypeStruct((T, D), _bf16)],
        scratch_shapes=[pltpu.VMEM((HALO + TM_IN, C), _f32), pltpu.VMEM((TM_IN, C), _f32),
                        pltpu.VMEM((TM_IN, D), _bf16)],
        compiler_params=pltpu.CompilerParams(dimension_semantics=("arbitrary",),
                                             vmem_limit_bytes=VMEM_LIMIT),
        name="inproj",
    )(x2, pos2, invf4, row(norm_mix_pre[0]), w_small, w_big, row(q_norm[0]), w_uq_k, row(kv_norm[0]),
      w_uk_k, w_uvt_k, cw, row(conv_b[0]), row(conv_ln_g[0]), row(conv_ln_b[0]),
      w_pw2[0].astype(_bf16), row(b_pw2[0]))

    n_blocks, tables = _attn_stream(S)
    assert n_blocks % (2 * ATT_UNROLL) == 0
    n_iter = n_blocks // ATT_UNROLL + 2
    mask_tiles = _attn_mask_tiles()
    qk_spec = pl.BlockSpec((S, ATT_HEADS * HEAD_SLOT), lambda b, g, *_: (b, g))
    attn = pl.pallas_call(
        functools.partial(_attn_kernel, n_iter=n_iter),
        grid_spec=pltpu.PrefetchScalarGridSpec(
            num_scalar_prefetch=len(tables),
            grid=(B, H // ATT_HEADS),
            in_specs=[qk_spec, qk_spec,
                      pl.BlockSpec((None, ATT_HEADS * V_HEAD_DIM, S), lambda b, g, *_: (b, g, 0)),
                      pl.BlockSpec(mask_tiles.shape, lambda b, g, *_: (0, 0, 0))],
            out_specs=pl.BlockSpec((S, ATT_HEADS * V_HEAD_DIM), lambda b, g, *_: (b, g)),
            scratch_shapes=[pltpu.VMEM((ATT_HEADS, 2 * ATT_UNROLL, TK, TQ), _f32),
                            pltpu.VMEM((ATT_HEADS, 2 * ATT_UNROLL, TK, TQ), _bf16),
                            pltpu.VMEM((ATT_HEADS, S // TQ + 1, V_HEAD_DIM, TQ), _f32),
                            pltpu.VMEM((ATT_HEADS, S // TQ + 1, 1, TQ), _f32)]),
        out_shape=jax.ShapeDtypeStruct((T, H * V_HEAD_DIM), _bf16),
        compiler_params=pltpu.CompilerParams(dimension_semantics=("arbitrary", "arbitrary"),
                                             vmem_limit_bytes=VMEM_LIMIT),
        name="attn",
    )(*[jnp.asarray(t) for t in tables], q, k, vt, jnp.asarray(mask_tiles))

    out = pl.pallas_call(
        _post_kernel,
        grid=(T // TM_POST,),
        in_specs=[_rows(TM_POST, H * V_HEAD_DIM), _rows(TM_POST, D), _rows(TM_POST, D),
                  _rows(TM_POST, D), _resident((H * V_HEAD_DIM, D)), _resident((D, D)),
                  _resident((1, D)), _resident((1, D)), _resident((D, D_FF)),
                  _resident((D_FF, D)), _resident((1, D))],
        out_specs=_rows(TM_POST, D),
        out_shape=jax.ShapeDtypeStruct((T, D), _f32),
        compiler_params=pltpu.CompilerParams(dimension_semantics=("arbitrary",),
                                             vmem_limit_bytes=VMEM_LIMIT),
        name="post",
    )(attn, sg, gc, x2, w_o_attn[0].astype(_bf16), w_out[0].astype(_bf16),
      row(norm_mix_post[0]), row(norm_mlp_pre[0]), w_ff1[0].astype(_bf16),
      w_ff2[0].astype(_bf16), row(norm_mlp_post[0]))
    return out.reshape(B, S, D)
```

```python
import functools
import math

import jax
import jax.numpy as jnp
import numpy as np
from jax import lax
from jax.experimental import pallas as pl
from jax.experimental.pallas import tpu as pltpu

D_MODEL = 1024
N_HEADS = 8
QK_NOPE_DIM = 128
QK_ROPE_DIM = 64
V_HEAD_DIM = 128
Q_LORA_RANK = 384
KV_LORA_RANK = 256
ROPE_THETA = 10000.0
CONV_CHANNELS = 1024
CONV_WIDTH = 31
D_FF = 4096
EPS = 1e-6

HALF_ROPE = QK_ROPE_DIM // 2
ROPE_PACK = 4
ROPE_LANES = ROPE_PACK * HALF_ROPE
HEAD_SLOT = QK_NOPE_DIM + ROPE_LANES
SMALL_COLS = Q_LORA_RANK + KV_LORA_RANK + ROPE_LANES
HALO = 32
SUBLANES = 8
LANES = 128
CONV_ROWS = 64

Q_SCALE = (QK_NOPE_DIM + QK_ROPE_DIM) ** -0.5 * math.log2(math.e)
MASK_VALUE = -1e30

TM_IN = 512
TM_POST = 512
TQ = 256
TK = 512
ATT_HEADS = 2
ATT_UNROLL = 2
SUM_ROWS = 16
FF_CHUNK = 1024
POST_SPLIT = 2

VMEM_LIMIT = 60 * 1024 * 1024

_bf16 = jnp.bfloat16
_f32 = jnp.float32
_NT = (((1,), (1,)), ((), ()))


def _dot(a, b):
    return jnp.dot(a, b, preferred_element_type=_f32)


def _rms(x, g):
    return x * lax.rsqrt(jnp.mean(x * x, axis=-1, keepdims=True) + EPS) * g


def _inproj_kernel(x_ref, pos_ref, invf_ref, g_pre_ref, w_small_ref, w_big_ref, qn_g_ref, w_uq_ref,
                   kvn_g_ref, w_uk_ref, w_uvt_ref, cw_ref, cb_ref, lng_ref, lnb_ref,
                   w_pw2_ref, b_pw2_ref,
                   q_ref, k_ref, vt_ref, sg_ref, gc_ref,
                   ubuf, cacc, hbuf, *, tiles_per_seq):
    tm = x_ref.shape[0]
    C = CONV_CHANNELS

    @pl.when(pl.program_id(0) % tiles_per_seq == 0)
    def _():
        ubuf[0:HALO, :] = jnp.zeros((HALO, C), _f32)

    hbuf[...] = _rms(x_ref[...], g_pre_ref[...]).astype(_bf16)
    a = _dot(hbuf[...], w_big_ref[:, 0:C])
    b = _dot(hbuf[...], w_big_ref[:, C:2 * C])
    ubuf[HALO:HALO + tm, :] = a * jax.nn.sigmoid(b)

    shift = HALO - (CONV_WIDTH - 1)
    win = CONV_ROWS + HALO

    def conv_lanes(c, carry):
        l0 = pl.multiple_of(c * LANES, LANES)
        for r0 in range(0, tm, CONV_ROWS):
            window = ubuf[r0:r0 + win, pl.ds(l0, LANES)]
            acc = jnp.broadcast_to(cb_ref[:, pl.ds(l0, LANES)], (CONV_ROWS, LANES))
            for phase in range(SUBLANES):
                rolled = window if phase == 0 else pltpu.roll(window, win - phase, 0)
                for a in range(win // SUBLANES):
                    t = SUBLANES * a + phase - shift
                    if 0 <= t < CONV_WIDTH:
                        acc = acc + (rolled[SUBLANES * a:SUBLANES * a + CONV_ROWS, :]
                                     * cw_ref[pl.ds(t, 1), pl.ds(l0, LANES)])
            cacc[r0:r0 + CONV_ROWS, pl.ds(l0, LANES)] = acc
        return carry

    lax.fori_loop(0, C // LANES, conv_lanes, 0)
    ubuf[0:HALO, :] = ubuf[tm:tm + HALO, :]

    conv = cacc[...]
    xc = conv - jnp.mean(conv, axis=-1, keepdims=True)
    var = jnp.mean(xc * xc, axis=-1, keepdims=True)
    y = xc * lax.rsqrt(var + EPS) * lng_ref[...] + lnb_ref[...]
    swish = (y * jax.nn.sigmoid(y)).astype(_bf16)

    h = hbuf[...]
    sg_ref[...] = jax.nn.sigmoid(_dot(h, w_big_ref[:, 2 * C:3 * C])).astype(_bf16)
    g_conv = jax.nn.sigmoid(_dot(h, w_big_ref[:, 3 * C:4 * C]))

    zs = _dot(h, w_small_ref[...])
    cq = zs[:, 0:Q_LORA_RANK]
    ckv = zs[:, Q_LORA_RANK:Q_LORA_RANK + KV_LORA_RANK]
    kr4 = zs[:, Q_LORA_RANK + KV_LORA_RANK:SMALL_COLS]

    pos = pos_ref[...].astype(_f32)
    lane = lax.broadcasted_iota(jnp.int32, (tm // ROPE_PACK, ROPE_LANES), 1)
    in_group = [lane < (g + 1) * HALF_ROPE for g in range(ROPE_PACK - 1)]

    def by_group(parts):
        out = parts[-1]
        for g in reversed(range(ROPE_PACK - 1)):
            out = jnp.where(in_group[g], parts[g], out)
        return out

    ang = by_group([pos[:, g:g + 1] for g in range(ROPE_PACK)]) * invf_ref[...]
    cos, sin = jnp.cos(ang), jnp.sin(ang)
    cos_rot = [cos] + [pltpu.roll(cos, g * HALF_ROPE, 1) for g in range(1, ROPE_PACK)]
    sin_rot = [sin] + [pltpu.roll(sin, g * HALF_ROPE, 1) for g in range(1, ROPE_PACK)]
    rope4 = jnp.concatenate(
        [by_group([cos_rot[(0 - g) % ROPE_PACK], cos_rot[(1 - g) % ROPE_PACK],
                   -sin_rot[(2 - g) % ROPE_PACK], sin_rot[(3 - g) % ROPE_PACK]])
         for g in range(ROPE_PACK)], axis=0)

    q = _dot(_rms(cq, qn_g_ref[...]).astype(_bf16), w_uq_ref[...])
    rope4_q = rope4 * Q_SCALE
    for hd in range(N_HEADS):
        lo = hd * HEAD_SLOT
        q_ref[:, lo:lo + QK_NOPE_DIM] = (q[:, lo:lo + QK_NOPE_DIM] * Q_SCALE).astype(_bf16)
        q_ref[:, lo + QK_NOPE_DIM:lo + HEAD_SLOT] = (
            q[:, lo + QK_NOPE_DIM:lo + HEAD_SLOT] * rope4_q).astype(_bf16)

    kp = kr4 * rope4
    kdup = (kp + pltpu.roll(kp, 2 * HALF_ROPE, 1)).astype(_bf16)
    ckvn = _rms(ckv, kvn_g_ref[...]).astype(_bf16)
    kn = _dot(ckvn, w_uk_ref[...])
    for hd in range(N_HEADS):
        lo = hd * HEAD_SLOT
        k_ref[:, lo:lo + QK_NOPE_DIM] = kn[:, hd * QK_NOPE_DIM:(hd + 1) * QK_NOPE_DIM].astype(_bf16)
        k_ref[:, lo + QK_NOPE_DIM:lo + HEAD_SLOT] = kdup
    vt_ref[...] = lax.dot_general(w_uvt_ref[...], ckvn, _NT,
                                  preferred_element_type=_f32).astype(_bf16)

    y_conv = _dot(swish, w_pw2_ref[...]) + b_pw2_ref[...]
    gc_ref[...] = (g_conv * y_conv).astype(_bf16)


def _attn_stream(seq):
    qi, kj, first, last, kind = [], [], [], [], []
    for i in range(seq // TQ):
        n_blocks = (i * TQ + TQ - 1) // TK + 1
        for j in range(n_blocks):
            qi.append(i)
            kj.append(j)
            first.append(int(j == 0))
            last.append(int(j == n_blocks - 1))
            kind.append(0 if j < n_blocks - 1 else 1 + (i * TQ - j * TK) // TQ)
    pad = [0] * (2 * ATT_UNROLL)
    tables = [np.asarray(pad + t + pad, np.int32) for t in (qi, kj, first, last, kind)]
    return len(qi), tables


def _attn_mask_tiles():
    kk = np.arange(TK)[:, None]
    qq = np.arange(TQ)[None, :]
    tiles = [np.zeros((TK, TQ), np.float32)]
    for d in range(TK // TQ):
        tiles.append(np.where(kk <= qq + d * TQ, 0.0, MASK_VALUE).astype(np.float32))
    return np.stack(tiles)


def _attn_kernel(qi_ref, kj_ref, first_ref, last_ref, kind_ref,
                 q_ref, k_ref, vt_ref, mask_ref, o_ref, s_buf, p_buf, *, n_iter):
    U = ATT_UNROLL
    s_buf[...] = jnp.zeros(s_buf.shape, _f32)
    p_buf[...] = jnp.zeros(p_buf.shape, _bf16)

    def head_cols(ref, hd, width, rows):
        return ref[rows, hd * width:(hd + 1) * width]

    def stage_a(hd, e, slot):
        q0 = pl.multiple_of(qi_ref[e] * TQ, TQ)
        k0 = pl.multiple_of(kj_ref[e] * TK, TK)
        s = lax.dot_general(head_cols(k_ref, hd, HEAD_SLOT, pl.ds(k0, TK)),
                            head_cols(q_ref, hd, HEAD_SLOT, pl.ds(q0, TQ)), _NT,
                            preferred_element_type=_f32)
        s = s + mask_ref[kind_ref[e]]
        s_buf[hd, slot] = s
        return jnp.max(s, axis=0, keepdims=True)

    def stage_b(hd, e, slot, m, cmax):
        m_prev = jnp.where(first_ref[e] != 0, -jnp.inf, m)
        m_new = jnp.maximum(m_prev, cmax)
        p_buf[hd, slot] = jnp.exp2(s_buf[hd, slot] - m_new).astype(_bf16)
        return m_new, jnp.exp2(m_prev - m_new)

    ones_rows = jnp.ones((SUM_ROWS, TK), _bf16)

    def stage_c(hd, e, slot, acc, l, alpha):
        k0 = pl.multiple_of(kj_ref[e] * TK, TK)
        vt = vt_ref[hd * V_HEAD_DIM:(hd + 1) * V_HEAD_DIM, pl.ds(k0, TK)]
        pv = _dot(jnp.concatenate([vt, ones_rows], axis=0), p_buf[hd, slot])
        return alpha * acc + pv[:V_HEAD_DIM], alpha * l + pv[V_HEAD_DIM:V_HEAD_DIM + 1]

    def iteration(it, parity, carry):
        state = [dict(c) for c in carry]
        alpha_new = [list(st["alpha"]) for st in state]
        for u in range(U):
            for hd in range(ATT_HEADS):
                st = state[hd]
                st["m"], alpha_new[hd][u] = stage_b(hd, (it + 1) * U + u, (1 - parity) * U + u,
                                                    st["m"], st["cmax"][u])
        cmax_new = [[None] * U for _ in range(ATT_HEADS)]
        for u in range(U):
            for hd in range(ATT_HEADS):
                cmax_new[hd][u] = stage_a(hd, (it + 2) * U + u, parity * U + u)
        done = []
        for u in range(U):
            for hd in range(ATT_HEADS):
                st = state[hd]
                e = it * U + u
                st["acc"], st["l"] = stage_c(hd, e, parity * U + u, st["acc"], st["l"],
                                             st["alpha"][u])
                done.append((hd, e, st["acc"], st["l"]))
        for hd in range(ATT_HEADS):
            state[hd]["alpha"] = tuple(alpha_new[hd])
            state[hd]["cmax"] = tuple(cmax_new[hd])
        for hd, e, acc, l in done:
            @pl.when(last_ref[e] != 0)
            def _():
                q0 = pl.multiple_of(qi_ref[e] * TQ, TQ)
                o_ref[pl.ds(q0, TQ), hd * V_HEAD_DIM:(hd + 1) * V_HEAD_DIM] = (
                    (acc / l).T.astype(_bf16))
        return tuple(state)

    def two_iterations(i, carry):
        return iteration(2 * i + 1, 1, iteration(2 * i, 0, carry))

    zeros = jnp.zeros((1, TQ), _f32)
    init = dict(m=zeros, l=zeros, acc=jnp.zeros((V_HEAD_DIM, TQ), _f32),
                alpha=(zeros,) * U, cmax=(zeros,) * U)
    lax.fori_loop(0, n_iter // 2, two_iterations, (init,) * ATT_HEADS)


def _post_kernel(attn_ref, sg_ref, gc_ref, x_ref, w_o_ref, w_out_ref, g_mix_post_ref,
                 g_mlp_pre_ref, w_ff1_ref, w_ff2_ref, g_mlp_post_ref, out_ref):
    tm = x_ref.shape[0]
    halves = [pl.ds(i * (tm // POST_SPLIT), tm // POST_SPLIT) for i in range(POST_SPLIT)]

    merged = []
    for rows in halves:
        y_attn = _dot(attn_ref[rows, :], w_o_ref[...])
        merged.append((sg_ref[rows, :].astype(_f32) * y_attn
                       + gc_ref[rows, :].astype(_f32)).astype(_bf16))
    x1, h2 = [], []
    for rows, mg in zip(halves, merged):
        x1.append(x_ref[rows, :] + _rms(_dot(mg, w_out_ref[...]), g_mix_post_ref[...]))
        h2.append(_rms(x1[-1], g_mlp_pre_ref[...]).astype(_bf16))
    f = [jnp.zeros(x.shape, _f32) for x in x1]
    for c in range(D_FF // FF_CHUNK):
        cols = slice(c * FF_CHUNK, (c + 1) * FF_CHUNK)
        for i in range(POST_SPLIT):
            hc = jnp.maximum(_dot(h2[i], w_ff1_ref[:, cols]), 0.0)
            f[i] = f[i] + _dot((hc * hc).astype(_bf16), w_ff2_ref[cols, :])
    for rows, xi, fi in zip(halves, x1, f):
        out_ref[rows, :] = xi + _rms(fi, g_mlp_post_ref[...])


def _resident(shape):
    return pl.BlockSpec(shape, lambda *_: (0,) * len(shape), pipeline_mode=pl.Buffered(1))


def _rows(tm, cols):
    return pl.BlockSpec((tm, cols), lambda i: (i, 0))


def kernel(x, positions, norm_mix_pre, w_in, q_norm, w_uq, kv_norm, w_uk, w_uv, w_o_attn,
           conv_w, conv_b, conv_ln_g, conv_ln_b, w_pw2, b_pw2, w_out, norm_mix_post,
           norm_mlp_pre, w_ff1, w_ff2, norm_mlp_post):
    B, S, D = x.shape
    T = B * S
    assert D == D_MODEL and w_in.shape[0] == 1
    assert S % TM_IN == 0 and S % TK == 0 and TK % TQ == 0 and T % TM_POST == 0
    assert N_HEADS % ATT_HEADS == 0
    C = CONV_CHANNELS
    H = N_HEADS
    tiles_per_seq = S // TM_IN

    w_in0 = w_in[0].astype(_bf16)
    o_kv, o_kr = Q_LORA_RANK + KV_LORA_RANK, Q_LORA_RANK + KV_LORA_RANK + QK_ROPE_DIM
    k1 = w_in0[:, o_kv:o_kv + HALF_ROPE]
    k2 = w_in0[:, o_kv + HALF_ROPE:o_kr]
    w_small = jnp.concatenate([w_in0[:, :o_kv], k1, k2, k2, k1], axis=1)
    w_big = w_in0[:, o_kr:]

    wq = w_uq[0].reshape(Q_LORA_RANK, H, QK_NOPE_DIM + QK_ROPE_DIM)
    r1 = wq[:, :, QK_NOPE_DIM:QK_NOPE_DIM + HALF_ROPE]
    r2 = wq[:, :, QK_NOPE_DIM + HALF_ROPE:]
    w_uq_k = jnp.concatenate([wq[:, :, :QK_NOPE_DIM], r1, r2, r2, r1], axis=2)
    w_uq_k = w_uq_k.reshape(Q_LORA_RANK, H * HEAD_SLOT).astype(_bf16)
    w_uk_k = w_uk[0].reshape(KV_LORA_RANK, H * QK_NOPE_DIM).astype(_bf16)
    w_uvt_k = w_uv[0].reshape(KV_LORA_RANK, H * V_HEAD_DIM).T.astype(_bf16)
    cw = jnp.pad(conv_w[0].reshape(CONV_WIDTH, C), ((0, 1), (0, 0)))

    inv_freq = ROPE_THETA ** (-jnp.arange(0, QK_ROPE_DIM, 2, dtype=_f32) / QK_ROPE_DIM)
    invf4 = jnp.tile(inv_freq, 4).reshape(1, ROPE_LANES)

    row = lambda a: a.reshape(1, -1)
    x2 = x.reshape(T, D)
    pos2 = positions.reshape(T // TM_IN, ROPE_PACK, TM_IN // ROPE_PACK).transpose(0, 2, 1)

    q, k, vt, sg, gc = pl.pallas_call(
        functools.partial(_inproj_kernel, tiles_per_seq=tiles_per_seq),
        grid=(T // TM_IN,),
        in_specs=[
            _rows(TM_IN, D),
            pl.BlockSpec((None, TM_IN // ROPE_PACK, ROPE_PACK), lambda i: (i, 0, 0)),
            _resident((1, ROPE_LANES)), _resident((1, D)),
            _resident(w_small.shape), _resident(w_big.shape),
            _resident((1, Q_LORA_RANK)), _resident(w_uq_k.shape),
            _resident((1, KV_LORA_RANK)), _resident(w_uk_k.shape), _resident(w_uvt_k.shape),
            _resident(cw.shape), _resident((1, C)), _resident((1, C)), _resident((1, C)),
            _resident((C, D)), _resident((1, D)),
        ],
        out_specs=[_rows(TM_IN, H * HEAD_SLOT), _rows(TM_IN, H * HEAD_SLOT),
                   pl.BlockSpec((None, H * V_HEAD_DIM, TM_IN),
                                lambda i: (i // tiles_per_seq, 0, i % tiles_per_seq)),
                   _rows(TM_IN, D), _rows(TM_IN, D)],
        out_shape=[jax.ShapeDtypeStruct((T, H * HEAD_SLOT), _bf16),
                   jax.ShapeDtypeStruct((T, H * HEAD_SLOT), _bf16),
                   jax.ShapeDtypeStruct((B, H * V_HEAD_DIM, S), _bf16),
                   jax.ShapeDtypeStruct((T, D), _bf16),
                   jax.ShapeDtypeStruct((T, D), _bf16)],
        scratch_shapes=[pltpu.VMEM((HALO + TM_IN, C), _f32), pltpu.VMEM((TM_IN, C), _f32),
                        pltpu.VMEM((TM_IN, D), _bf16)],
        compiler_params=pltpu.CompilerParams(dimension_semantics=("arbitrary",),
                                             vmem_limit_bytes=VMEM_LIMIT),
        name="inproj",
    )(x2, pos2, invf4, row(norm_mix_pre[0]), w_small, w_big, row(q_norm[0]), w_uq_k, row(kv_norm[0]),
      w_uk_k, w_uvt_k, cw, row(conv_b[0]), row(conv_ln_g[0]), row(conv_ln_b[0]),
      w_pw2[0].astype(_bf16), row(b_pw2[0]))

    n_blocks, tables = _attn_stream(S)
    assert n_blocks % (2 * ATT_UNROLL) == 0
    n_iter = n_blocks // ATT_UNROLL + 2
    mask_tiles = _attn_mask_tiles()
    qk_spec = pl.BlockSpec((S, ATT_HEADS * HEAD_SLOT), lambda b, g, *_: (b, g))
    attn = pl.pallas_call(
        functools.partial(_attn_kernel, n_iter=n_iter),
        grid_spec=pltpu.PrefetchScalarGridSpec(
            num_scalar_prefetch=len(tables),
            grid=(B, H // ATT_HEADS),
            in_specs=[qk_spec, qk_spec,
                      pl.BlockSpec((None, ATT_HEADS * V_HEAD_DIM, S), lambda b, g, *_: (b, g, 0)),
                      pl.BlockSpec(mask_tiles.shape, lambda b, g, *_: (0, 0, 0))],
            out_specs=pl.BlockSpec((S, ATT_HEADS * V_HEAD_DIM), lambda b, g, *_: (b, g)),
            scratch_shapes=[pltpu.VMEM((ATT_HEADS, 2 * ATT_UNROLL, TK, TQ), _f32),
                            pltpu.VMEM((ATT_HEADS, 2 * ATT_UNROLL, TK, TQ), _bf16)]),
        out_shape=jax.ShapeDtypeStruct((T, H * V_HEAD_DIM), _bf16),
        compiler_params=pltpu.CompilerParams(dimension_semantics=("arbitrary", "arbitrary"),
                                             vmem_limit_bytes=VMEM_LIMIT),
        name="attn",
    )(*[jnp.asarray(t) for t in tables], q, k, vt, jnp.asarray(mask_tiles))

    out = pl.pallas_call(
        _post_kernel,
        grid=(T // TM_POST,),
        in_specs=[_rows(TM_POST, H * V_HEAD_DIM), _rows(TM_POST, D), _rows(TM_POST, D),
                  _rows(TM_POST, D), _resident((H * V_HEAD_DIM, D)), _resident((D, D)),
                  _resident((1, D)), _resident((1, D)), _resident((D, D_FF)),
                  _resident((D_FF, D)), _resident((1, D))],
        out_specs=_rows(TM_POST, D),
        out_shape=jax.ShapeDtypeStruct((T, D), _f32),
        compiler_params=pltpu.CompilerParams(dimension_semantics=("arbitrary",),
                                             vmem_limit_bytes=VMEM_LIMIT),
        name="post",
    )(attn, sg, gc, x2, w_o_attn[0].astype(_bf16), w_out[0].astype(_bf16),
      row(norm_mix_post[0]), row(norm_mlp_pre[0]), w_ff1[0].astype(_bf16),
      w_ff2[0].astype(_bf16), row(norm_mlp_post[0]))
    return out.reshape(B, S, D)
```

```python
import functools
import math

import jax
import jax.numpy as jnp
import numpy as np
from jax import lax
from jax.experimental import pallas as pl
from jax.experimental.pallas import tpu as pltpu

D_MODEL = 1024
N_HEADS = 8
QK_NOPE_DIM = 128
QK_ROPE_DIM = 64
V_HEAD_DIM = 128
Q_LORA_RANK = 384
KV_LORA_RANK = 256
ROPE_THETA = 10000.0
CONV_CHANNELS = 1024
CONV_WIDTH = 31
D_FF = 4096
EPS = 1e-6

HALF_ROPE = QK_ROPE_DIM // 2
ROPE_PACK = 4
ROPE_LANES = ROPE_PACK * HALF_ROPE
HEAD_SLOT = QK_NOPE_DIM + ROPE_LANES
SMALL_COLS = Q_LORA_RANK + KV_LORA_RANK + ROPE_LANES
HALO = 32
SUBLANES = 8
LANES = 128
CONV_ROWS = 64
CONV_STRIDE = 2

Q_SCALE = (QK_NOPE_DIM + QK_ROPE_DIM) ** -0.5 * math.log2(math.e)
MASK_VALUE = -1e30

TM_IN = 512
TM_POST = 512
TQ = 256
TK = 512
ATT_HEADS = 2
ATT_UNROLL = 2
SUM_ROWS = 16
FF_CHUNK = 1024
POST_SPLIT = 2

VMEM_LIMIT = 60 * 1024 * 1024

_bf16 = jnp.bfloat16
_f32 = jnp.float32
_NT = (((1,), (1,)), ((), ()))


def _dot(a, b):
    return jnp.dot(a, b, preferred_element_type=_f32)


def _rms(x, g):
    return x * lax.rsqrt(jnp.mean(x * x, axis=-1, keepdims=True) + EPS) * g


def _inproj_kernel(x_ref, pos_ref, invf_ref, g_pre_ref, w_small_ref, w_big_ref, qn_g_ref, w_uq_ref,
                   kvn_g_ref, w_uk_ref, w_uvt_ref, cw_ref, cb_ref, lng_ref, lnb_ref,
                   w_pw2_ref, b_pw2_ref,
                   q_ref, k_ref, vt_ref, sg_ref, gc_ref,
                   ubuf, cacc, hbuf, *, tiles_per_seq):
    tm = x_ref.shape[0]
    C = CONV_CHANNELS

    n_slabs = C // LANES

    @pl.when(pl.program_id(0) % tiles_per_seq == 0)
    def _():
        ubuf[:, 0:HALO, :] = jnp.zeros((n_slabs, HALO, LANES), _f32)

    hbuf[...] = _rms(x_ref[...], g_pre_ref[...]).astype(_bf16)
    a = _dot(hbuf[...], w_big_ref[:, 0:C])
    b = _dot(hbuf[...], w_big_ref[:, C:2 * C])
    u = a * jax.nn.sigmoid(b)
    for c in range(n_slabs):
        ubuf[c, HALO:HALO + tm, :] = u[:, c * LANES:(c + 1) * LANES]

    shift = HALO - (CONV_WIDTH - 1)
    group = CONV_STRIDE * SUBLANES

    def conv_slab(c, carry):
        for r0 in range(0, tm, CONV_ROWS):
            starts = [r0 + g * group + p for g in range(CONV_ROWS // group)
                      for p in range(CONV_STRIDE)]
            accs = [jnp.broadcast_to(cb_ref[c], (SUBLANES, LANES)) for _ in starts]
            for t in range(CONV_WIDTH):
                w = cw_ref[c, pl.ds(t, 1), :]
                for j, start in enumerate(starts):
                    rows = pl.ds(start + shift + t, SUBLANES, stride=CONV_STRIDE)
                    accs[j] = accs[j] + ubuf[c, rows, :] * w
            for acc, start in zip(accs, starts):
                cacc[c, pl.ds(start, SUBLANES, stride=CONV_STRIDE), :] = acc
        return carry

    lax.fori_loop(0, n_slabs, conv_slab, 0)
    ubuf[:, 0:HALO, :] = ubuf[:, tm:tm + HALO, :]

    conv = jnp.concatenate([cacc[c] for c in range(n_slabs)], axis=1)
    xc = conv - jnp.mean(conv, axis=-1, keepdims=True)
    var = jnp.mean(xc * xc, axis=-1, keepdims=True)
    y = xc * lax.rsqrt(var + EPS) * lng_ref[...] + lnb_ref[...]
    swish = (y * jax.nn.sigmoid(y)).astype(_bf16)

    h = hbuf[...]
    sg_ref[...] = jax.nn.sigmoid(_dot(h, w_big_ref[:, 2 * C:3 * C])).astype(_bf16)
    g_conv = jax.nn.sigmoid(_dot(h, w_big_ref[:, 3 * C:4 * C]))

    zs = _dot(h, w_small_ref[...])
    cq = zs[:, 0:Q_LORA_RANK]
    ckv = zs[:, Q_LORA_RANK:Q_LORA_RANK + KV_LORA_RANK]
    kr4 = zs[:, Q_LORA_RANK + KV_LORA_RANK:SMALL_COLS]

    pos = pos_ref[...].astype(_f32)
    lane = lax.broadcasted_iota(jnp.int32, (tm // ROPE_PACK, ROPE_LANES), 1)
    in_group = [lane < (g + 1) * HALF_ROPE for g in range(ROPE_PACK - 1)]

    def by_group(parts):
        out = parts[-1]
        for g in reversed(range(ROPE_PACK - 1)):
            out = jnp.where(in_group[g], parts[g], out)
        return out

    ang = by_group([pos[:, g:g + 1] for g in range(ROPE_PACK)]) * invf_ref[...]
    cos, sin = jnp.cos(ang), jnp.sin(ang)
    cos_rot = [cos] + [pltpu.roll(cos, g * HALF_ROPE, 1) for g in range(1, ROPE_PACK)]
    sin_rot = [sin] + [pltpu.roll(sin, g * HALF_ROPE, 1) for g in range(1, ROPE_PACK)]
    rope4 = jnp.concatenate(
        [by_group([cos_rot[(0 - g) % ROPE_PACK], cos_rot[(1 - g) % ROPE_PACK],
                   -sin_rot[(2 - g) % ROPE_PACK], sin_rot[(3 - g) % ROPE_PACK]])
         for g in range(ROPE_PACK)], axis=0)

    q = _dot(_rms(cq, qn_g_ref[...]).astype(_bf16), w_uq_ref[...])
    rope4_q = rope4 * Q_SCALE
    for hd in range(N_HEADS):
        lo = hd * HEAD_SLOT
        q_ref[:, lo:lo + QK_NOPE_DIM] = (q[:, lo:lo + QK_NOPE_DIM] * Q_SCALE).astype(_bf16)
        q_ref[:, lo + QK_NOPE_DIM:lo + HEAD_SLOT] = (
            q[:, lo + QK_NOPE_DIM:lo + HEAD_SLOT] * rope4_q).astype(_bf16)

    kp = kr4 * rope4
    kdup = (kp + pltpu.roll(kp, 2 * HALF_ROPE, 1)).astype(_bf16)
    ckvn = _rms(ckv, kvn_g_ref[...]).astype(_bf16)
    kn = _dot(ckvn, w_uk_ref[...])
    for hd in range(N_HEADS):
        lo = hd * HEAD_SLOT
        k_ref[:, lo:lo + QK_NOPE_DIM] = kn[:, hd * QK_NOPE_DIM:(hd + 1) * QK_NOPE_DIM].astype(_bf16)
        k_ref[:, lo + QK_NOPE_DIM:lo + HEAD_SLOT] = kdup
    vt_ref[...] = lax.dot_general(w_uvt_ref[...], ckvn, _NT,
                                  preferred_element_type=_f32).astype(_bf16)

    y_conv = _dot(swish, w_pw2_ref[...]) + b_pw2_ref[...]
    gc_ref[...] = (g_conv * y_conv).astype(_bf16)


def _attn_stream(seq):
    qi, kj, first, last, kind = [], [], [], [], []
    for i in range(seq // TQ):
        n_blocks = (i * TQ + TQ - 1) // TK + 1
        for j in range(n_blocks):
            qi.append(i)
            kj.append(j)
            first.append(int(j == 0))
            last.append(int(j == n_blocks - 1))
            kind.append(0 if j < n_blocks - 1 else 1 + (i * TQ - j * TK) // TQ)
    pad = [0] * (2 * ATT_UNROLL)
    tables = [np.asarray(pad + t + pad, np.int32) for t in (qi, kj, first, last, kind)]
    return len(qi), tables


def _attn_mask_tiles():
    kk = np.arange(TK)[:, None]
    qq = np.arange(TQ)[None, :]
    tiles = [np.zeros((TK, TQ), np.float32)]
    for d in range(TK // TQ):
        tiles.append(np.where(kk <= qq + d * TQ, 0.0, MASK_VALUE).astype(np.float32))
    return np.stack(tiles)


def _attn_kernel(qi_ref, kj_ref, first_ref, last_ref, kind_ref,
                 q_ref, k_ref, vt_ref, mask_ref, o_ref, s_buf, p_buf, *, n_iter):
    U = ATT_UNROLL
    s_buf[...] = jnp.zeros(s_buf.shape, _f32)
    p_buf[...] = jnp.zeros(p_buf.shape, _bf16)

    def head_cols(ref, hd, width, rows):
        return ref[rows, hd * width:(hd + 1) * width]

    def stage_a(hd, e, slot):
        q0 = pl.multiple_of(qi_ref[e] * TQ, TQ)
        k0 = pl.multiple_of(kj_ref[e] * TK, TK)
        s = lax.dot_general(head_cols(k_ref, hd, HEAD_SLOT, pl.ds(k0, TK)),
                            head_cols(q_ref, hd, HEAD_SLOT, pl.ds(q0, TQ)), _NT,
                            preferred_element_type=_f32)
        s = s + mask_ref[kind_ref[e]]
        s_buf[hd, slot] = s
        return jnp.max(s, axis=0, keepdims=True)

    def stage_b(hd, e, slot, m, cmax):
        m_prev = jnp.where(first_ref[e] != 0, -jnp.inf, m)
        m_new = jnp.maximum(m_prev, cmax)
        p_buf[hd, slot] = jnp.exp2(s_buf[hd, slot] - m_new).astype(_bf16)
        return m_new, jnp.exp2(m_prev - m_new)

    ones_rows = jnp.ones((SUM_ROWS, TK), _bf16)

    def stage_c(hd, e, slot, acc, l, alpha):
        k0 = pl.multiple_of(kj_ref[e] * TK, TK)
        vt = vt_ref[hd * V_HEAD_DIM:(hd + 1) * V_HEAD_DIM, pl.ds(k0, TK)]
        pv = _dot(jnp.concatenate([vt, ones_rows], axis=0), p_buf[hd, slot])
        return alpha * acc + pv[:V_HEAD_DIM], alpha * l + pv[V_HEAD_DIM:V_HEAD_DIM + 1]

    def iteration(it, parity, carry):
        state = [dict(c) for c in carry]
        alpha_new = [list(st["alpha"]) for st in state]
        for u in range(U):
            for hd in range(ATT_HEADS):
                st = state[hd]
                st["m"], alpha_new[hd][u] = stage_b(hd, (it + 1) * U + u, (1 - parity) * U + u,
                                                    st["m"], st["cmax"][u])
        cmax_new = [[None] * U for _ in range(ATT_HEADS)]
        for u in range(U):
            for hd in range(ATT_HEADS):
                cmax_new[hd][u] = stage_a(hd, (it + 2) * U + u, parity * U + u)
        done = []
        for u in range(U):
            for hd in range(ATT_HEADS):
                st = state[hd]
                e = it * U + u
                st["acc"], st["l"] = stage_c(hd, e, parity * U + u, st["acc"], st["l"],
                                             st["alpha"][u])
                done.append((hd, e, st["acc"], st["l"]))
        for hd in range(ATT_HEADS):
            state[hd]["alpha"] = tuple(alpha_new[hd])
            state[hd]["cmax"] = tuple(cmax_new[hd])
        for hd, e, acc, l in done:
            @pl.when(last_ref[e] != 0)
            def _():
                q0 = pl.multiple_of(qi_ref[e] * TQ, TQ)
                o_ref[pl.ds(q0, TQ), hd * V_HEAD_DIM:(hd + 1) * V_HEAD_DIM] = (
                    (acc / l).T.astype(_bf16))
        return tuple(state)

    def two_iterations(i, carry):
        return iteration(2 * i + 1, 1, iteration(2 * i, 0, carry))

    zeros = jnp.zeros((1, TQ), _f32)
    init = dict(m=zeros, l=zeros, acc=jnp.zeros((V_HEAD_DIM, TQ), _f32),
                alpha=(zeros,) * U, cmax=(zeros,) * U)
    lax.fori_loop(0, n_iter // 2, two_iterations, (init,) * ATT_HEADS)


def _post_kernel(attn_ref, sg_ref, gc_ref, x_ref, w_o_ref, w_out_ref, g_mix_post_ref,
                 g_mlp_pre_ref, w_ff1_ref, w_ff2_ref, g_mlp_post_ref, out_ref):
    tm = x_ref.shape[0]
    halves = [pl.ds(i * (tm // POST_SPLIT), tm // POST_SPLIT) for i in range(POST_SPLIT)]

    merged = []
    for rows in halves:
        y_attn = _dot(attn_ref[rows, :], w_o_ref[...])
        merged.append((sg_ref[rows, :].astype(_f32) * y_attn
                       + gc_ref[rows, :].astype(_f32)).astype(_bf16))
    x1, h2 = [], []
    for rows, mg in zip(halves, merged):
        x1.append(x_ref[rows, :] + _rms(_dot(mg, w_out_ref[...]), g_mix_post_ref[...]))
        h2.append(_rms(x1[-1], g_mlp_pre_ref[...]).astype(_bf16))
    f = [jnp.zeros(x.shape, _f32) for x in x1]
    for c in range(D_FF // FF_CHUNK):
        cols = slice(c * FF_CHUNK, (c + 1) * FF_CHUNK)
        for i in range(POST_SPLIT):
            hc = jnp.maximum(_dot(h2[i], w_ff1_ref[:, cols]), 0.0)
            f[i] = f[i] + _dot((hc * hc).astype(_bf16), w_ff2_ref[cols, :])
    for rows, xi, fi in zip(halves, x1, f):
        out_ref[rows, :] = xi + _rms(fi, g_mlp_post_ref[...])


def _resident(shape):
    return pl.BlockSpec(shape, lambda *_: (0,) * len(shape), pipeline_mode=pl.Buffered(1))


def _rows(tm, cols):
    return pl.BlockSpec((tm, cols), lambda i: (i, 0))


def kernel(x, positions, norm_mix_pre, w_in, q_norm, w_uq, kv_norm, w_uk, w_uv, w_o_attn,
           conv_w, conv_b, conv_ln_g, conv_ln_b, w_pw2, b_pw2, w_out, norm_mix_post,
           norm_mlp_pre, w_ff1, w_ff2, norm_mlp_post):
    B, S, D = x.shape
    T = B * S
    assert D == D_MODEL and w_in.shape[0] == 1
    assert S % TM_IN == 0 and S % TK == 0 and TK % TQ == 0 and T % TM_POST == 0
    assert N_HEADS % ATT_HEADS == 0
    C = CONV_CHANNELS
    H = N_HEADS
    tiles_per_seq = S // TM_IN

    w_in0 = w_in[0].astype(_bf16)
    o_kv, o_kr = Q_LORA_RANK + KV_LORA_RANK, Q_LORA_RANK + KV_LORA_RANK + QK_ROPE_DIM
    k1 = w_in0[:, o_kv:o_kv + HALF_ROPE]
    k2 = w_in0[:, o_kv + HALF_ROPE:o_kr]
    w_small = jnp.concatenate([w_in0[:, :o_kv], k1, k2, k2, k1], axis=1)
    w_big = w_in0[:, o_kr:]

    wq = w_uq[0].reshape(Q_LORA_RANK, H, QK_NOPE_DIM + QK_ROPE_DIM)
    r1 = wq[:, :, QK_NOPE_DIM:QK_NOPE_DIM + HALF_ROPE]
    r2 = wq[:, :, QK_NOPE_DIM + HALF_ROPE:]
    w_uq_k = jnp.concatenate([wq[:, :, :QK_NOPE_DIM], r1, r2, r2, r1], axis=2)
    w_uq_k = w_uq_k.reshape(Q_LORA_RANK, H * HEAD_SLOT).astype(_bf16)
    w_uk_k = w_uk[0].reshape(KV_LORA_RANK, H * QK_NOPE_DIM).astype(_bf16)
    w_uvt_k = w_uv[0].reshape(KV_LORA_RANK, H * V_HEAD_DIM).T.astype(_bf16)
    cw = jnp.pad(conv_w[0].reshape(CONV_WIDTH, C), ((0, 1), (0, 0)))
    cw = cw.reshape(CONV_WIDTH + 1, C // LANES, LANES).transpose(1, 0, 2)
    cb = conv_b[0].reshape(C // LANES, 1, LANES)

    inv_freq = ROPE_THETA ** (-jnp.arange(0, QK_ROPE_DIM, 2, dtype=_f32) / QK_ROPE_DIM)
    invf4 = jnp.tile(inv_freq, 4).reshape(1, ROPE_LANES)

    row = lambda a: a.reshape(1, -1)
    x2 = x.reshape(T, D)
    pos2 = positions.reshape(T // TM_IN, ROPE_PACK, TM_IN // ROPE_PACK).transpose(0, 2, 1)

    q, k, vt, sg, gc = pl.pallas_call(
        functools.partial(_inproj_kernel, tiles_per_seq=tiles_per_seq),
        grid=(T // TM_IN,),
        in_specs=[
            _rows(TM_IN, D),
            pl.BlockSpec((None, TM_IN // ROPE_PACK, ROPE_PACK), lambda i: (i, 0, 0)),
            _resident((1, ROPE_LANES)), _resident((1, D)),
            _resident(w_small.shape), _resident(w_big.shape),
            _resident((1, Q_LORA_RANK)), _resident(w_uq_k.shape),
            _resident((1, KV_LORA_RANK)), _resident(w_uk_k.shape), _resident(w_uvt_k.shape),
            _resident(cw.shape), _resident(cb.shape), _resident((1, C)), _resident((1, C)),
            _resident((C, D)), _resident((1, D)),
        ],
        out_specs=[_rows(TM_IN, H * HEAD_SLOT), _rows(TM_IN, H * HEAD_SLOT),
                   pl.BlockSpec((None, H * V_HEAD_DIM, TM_IN),
                                lambda i: (i // tiles_per_seq, 0, i % tiles_per_seq)),
                   _rows(TM_IN, D), _rows(TM_IN, D)],
        out_shape=[jax.ShapeDtypeStruct((T, H * HEAD_SLOT), _bf16),
                   jax.ShapeDtypeStruct((T, H * HEAD_SLOT), _bf16),
                   jax.ShapeDtypeStruct((B, H * V_HEAD_DIM, S), _bf16),
                   jax.ShapeDtypeStruct((T, D), _bf16),
                   jax.ShapeDtypeStruct((T, D), _bf16)],
        scratch_shapes=[pltpu.VMEM((C // LANES, HALO + TM_IN, LANES), _f32),
                        pltpu.VMEM((C // LANES, TM_IN, LANES), _f32),
                        pltpu.VMEM((TM_IN, D), _bf16)],
        compiler_params=pltpu.CompilerParams(dimension_semantics=("arbitrary",),
                                             vmem_limit_bytes=VMEM_LIMIT),
        name="inproj",
    )(x2, pos2, invf4, row(norm_mix_pre[0]), w_small, w_big, row(q_norm[0]), w_uq_k, row(kv_norm[0]),
      w_uk_k, w_uvt_k, cw, cb, row(conv_ln_g[0]), row(conv_ln_b[0]),
      w_pw2[0].astype(_bf16), row(b_pw2[0]))

    n_blocks, tables = _attn_stream(S)
    assert n_blocks % (2 * ATT_UNROLL) == 0
    n_iter = n_blocks // ATT_UNROLL + 2
    mask_tiles = _attn_mask_tiles()
    qk_spec = pl.BlockSpec((S, ATT_HEADS * HEAD_SLOT), lambda b, g, *_: (b, g))
    attn = pl.pallas_call(
        functools.partial(_attn_kernel, n_iter=n_iter),
        grid_spec=pltpu.PrefetchScalarGridSpec(
            num_scalar_prefetch=len(tables),
            grid=(B, H // ATT_HEADS),
            in_specs=[qk_spec, qk_spec,
                      pl.BlockSpec((None, ATT_HEADS * V_HEAD_DIM, S), lambda b, g, *_: (b, g, 0)),
                      pl.BlockSpec(mask_tiles.shape, lambda b, g, *_: (0, 0, 0))],
            out_specs=pl.BlockSpec((S, ATT_HEADS * V_HEAD_DIM), lambda b, g, *_: (b, g)),
            scratch_shapes=[pltpu.VMEM((ATT_HEADS, 2 * ATT_UNROLL, TK, TQ), _f32),
                            pltpu.VMEM((ATT_HEADS, 2 * ATT_UNROLL, TK, TQ), _bf16)]),
        out_shape=jax.ShapeDtypeStruct((T, H * V_HEAD_DIM), _bf16),
        compiler_params=pltpu.CompilerParams(dimension_semantics=("arbitrary", "arbitrary"),
                                             vmem_limit_bytes=VMEM_LIMIT),
        name="attn",
    )(*[jnp.asarray(t) for t in tables], q, k, vt, jnp.asarray(mask_tiles))

    out = pl.pallas_call(
        _post_kernel,
        grid=(T // TM_POST,),
        in_specs=[_rows(TM_POST, H * V_HEAD_DIM), _rows(TM_POST, D), _rows(TM_POST, D),
                  _rows(TM_POST, D), _resident((H * V_HEAD_DIM, D)), _resident((D, D)),
                  _resident((1, D)), _resident((1, D)), _resident((D, D_FF)),
                  _resident((D_FF, D)), _resident((1, D))],
        out_specs=_rows(TM_POST, D),
        out_shape=jax.ShapeDtypeStruct((T, D), _f32),
        compiler_params=pltpu.CompilerParams(dimension_semantics=("arbitrary",),
                                             vmem_limit_bytes=VMEM_LIMIT),
        name="post",
    )(attn, sg, gc, x2, w_o_attn[0].astype(_bf16), w_out[0].astype(_bf16),
      row(norm_mix_post[0]), row(norm_mlp_pre[0]), w_ff1[0].astype(_bf16),
      w_ff2[0].astype(_bf16), row(norm_mlp_post[0]))
    return out.reshape(B, S, D)
```

```python
import functools
import math

import jax
import jax.numpy as jnp
import numpy as np
from jax import lax
from jax.experimental import pallas as pl
from jax.experimental.pallas import tpu as pltpu

D_MODEL = 1024
N_HEADS = 8
QK_NOPE_DIM = 128
QK_ROPE_DIM = 64
V_HEAD_DIM = 128
Q_LORA_RANK = 384
KV_LORA_RANK = 256
ROPE_THETA = 10000.0
CONV_CHANNELS = 1024
CONV_WIDTH = 31
D_FF = 4096
EPS = 1e-6

HALF_ROPE = QK_ROPE_DIM // 2
ROPE_PACK = 4
ROPE_LANES = ROPE_PACK * HALF_ROPE
HEAD_SLOT = QK_NOPE_DIM + ROPE_LANES
SMALL_COLS = Q_LORA_RANK + KV_LORA_RANK + ROPE_LANES
HALO = 32
SUBLANES = 8
LANES = 128
CONV_ROWS = 64
CONV_STRIDE = 2

Q_SCALE = (QK_NOPE_DIM + QK_ROPE_DIM) ** -0.5 * math.log2(math.e)
MASK_VALUE = -1e30

TM_IN = 512
TM_POST = 512
TQ = 256
TK = 512
ATT_HEADS = 2
ATT_UNROLL = 2
SUM_ROWS = 16
FF_CHUNK = 1024
POST_SPLIT = 2

VMEM_LIMIT = 60 * 1024 * 1024

_bf16 = jnp.bfloat16
_f32 = jnp.float32
_NT = (((1,), (1,)), ((), ()))


def _dot(a, b):
    return jnp.dot(a, b, preferred_element_type=_f32)


def _dot_nt(a, b):
    return lax.dot_general(a, b, _NT, preferred_element_type=_f32)


def _rms(x, g):
    return x * lax.rsqrt(jnp.mean(x * x, axis=-1, keepdims=True) + EPS) * g


def _inproj_kernel(x_ref, pos_ref, invf_ref, g_pre_ref, w_small_ref, w_big_ref, qn_g_ref, w_uq_ref,
                   kvn_g_ref, w_uk_ref, w_uvt_ref, cw_ref, cb_ref, lng_ref, lnb_ref,
                   w_pw2_ref, b_pw2_ref,
                   q_ref, k_ref, vt_ref, sg_ref, gc_ref,
                   ubuf, cacc, hbuf, *, tiles_per_seq):
    tm = x_ref.shape[0]
    C = CONV_CHANNELS

    n_slabs = C // LANES

    @pl.when(pl.program_id(0) % tiles_per_seq == 0)
    def _():
        ubuf[:, 0:HALO, :] = jnp.zeros((n_slabs, HALO, LANES), _f32)

    hbuf[...] = _rms(x_ref[...], g_pre_ref[...]).astype(_bf16)
    a = _dot_nt(hbuf[...], w_big_ref[0:C, :])
    b = _dot_nt(hbuf[...], w_big_ref[C:2 * C, :])
    u = a * jax.nn.sigmoid(b)
    for c in range(n_slabs):
        ubuf[c, HALO:HALO + tm, :] = u[:, c * LANES:(c + 1) * LANES]

    shift = HALO - (CONV_WIDTH - 1)
    group = CONV_STRIDE * SUBLANES

    def conv_slab(c, carry):
        for r0 in range(0, tm, CONV_ROWS):
            starts = [r0 + g * group + p for g in range(CONV_ROWS // group)
                      for p in range(CONV_STRIDE)]
            accs = [jnp.broadcast_to(cb_ref[c], (SUBLANES, LANES)) for _ in starts]
            for t in range(CONV_WIDTH):
                w = cw_ref[c, pl.ds(t, 1), :]
                for j, start in enumerate(starts):
                    rows = pl.ds(start + shift + t, SUBLANES, stride=CONV_STRIDE)
                    accs[j] = accs[j] + ubuf[c, rows, :] * w
            for acc, start in zip(accs, starts):
                cacc[c, pl.ds(start, SUBLANES, stride=CONV_STRIDE), :] = acc
        return carry

    lax.fori_loop(0, n_slabs, conv_slab, 0)
    ubuf[:, 0:HALO, :] = ubuf[:, tm:tm + HALO, :]

    conv = jnp.concatenate([cacc[c] for c in range(n_slabs)], axis=1)
    xc = conv - jnp.mean(conv, axis=-1, keepdims=True)
    var = jnp.mean(xc * xc, axis=-1, keepdims=True)
    y = xc * lax.rsqrt(var + EPS) * lng_ref[...] + lnb_ref[...]
    swish = (y * jax.nn.sigmoid(y)).astype(_bf16)

    h = hbuf[...]
    sg_ref[...] = jax.nn.sigmoid(_dot_nt(h, w_big_ref[2 * C:3 * C, :])).astype(_bf16)
    g_conv = jax.nn.sigmoid(_dot_nt(h, w_big_ref[3 * C:4 * C, :]))

    zs = _dot_nt(h, w_small_ref[...])
    cq = zs[:, 0:Q_LORA_RANK]
    ckv = zs[:, Q_LORA_RANK:Q_LORA_RANK + KV_LORA_RANK]
    kr4 = zs[:, Q_LORA_RANK + KV_LORA_RANK:SMALL_COLS]

    pos = pos_ref[...].astype(_f32)
    lane = lax.broadcasted_iota(jnp.int32, (tm // ROPE_PACK, ROPE_LANES), 1)
    in_group = [lane < (g + 1) * HALF_ROPE for g in range(ROPE_PACK - 1)]

    def by_group(parts):
        out = parts[-1]
        for g in reversed(range(ROPE_PACK - 1)):
            out = jnp.where(in_group[g], parts[g], out)
        return out

    ang = by_group([pos[:, g:g + 1] for g in range(ROPE_PACK)]) * invf_ref[...]
    cos, sin = jnp.cos(ang), jnp.sin(ang)
    cos_rot = [cos] + [pltpu.roll(cos, g * HALF_ROPE, 1) for g in range(1, ROPE_PACK)]
    sin_rot = [sin] + [pltpu.roll(sin, g * HALF_ROPE, 1) for g in range(1, ROPE_PACK)]
    rope4 = jnp.concatenate(
        [by_group([cos_rot[(0 - g) % ROPE_PACK], cos_rot[(1 - g) % ROPE_PACK],
                   -sin_rot[(2 - g) % ROPE_PACK], sin_rot[(3 - g) % ROPE_PACK]])
         for g in range(ROPE_PACK)], axis=0)

    q = _dot(_rms(cq, qn_g_ref[...]).astype(_bf16), w_uq_ref[...])
    rope4_q = rope4 * Q_SCALE
    for hd in range(N_HEADS):
        lo = hd * HEAD_SLOT
        q_ref[:, lo:lo + QK_NOPE_DIM] = (q[:, lo:lo + QK_NOPE_DIM] * Q_SCALE).astype(_bf16)
        q_ref[:, lo + QK_NOPE_DIM:lo + HEAD_SLOT] = (
            q[:, lo + QK_NOPE_DIM:lo + HEAD_SLOT] * rope4_q).astype(_bf16)

    kp = kr4 * rope4
    kdup = (kp + pltpu.roll(kp, 2 * HALF_ROPE, 1)).astype(_bf16)
    ckvn = _rms(ckv, kvn_g_ref[...]).astype(_bf16)
    kn = _dot(ckvn, w_uk_ref[...])
    for hd in range(N_HEADS):
        lo = hd * HEAD_SLOT
        k_ref[:, lo:lo + QK_NOPE_DIM] = kn[:, hd * QK_NOPE_DIM:(hd + 1) * QK_NOPE_DIM].astype(_bf16)
        k_ref[:, lo + QK_NOPE_DIM:lo + HEAD_SLOT] = kdup
    vt_ref[...] = _dot_nt(w_uvt_ref[...], ckvn).astype(_bf16)

    y_conv = _dot(swish, w_pw2_ref[...]) + b_pw2_ref[...]
    gc_ref[...] = (g_conv * y_conv).astype(_bf16)


def _attn_stream(seq):
    qi, kj, first, last, kind = [], [], [], [], []
    for i in range(seq // TQ):
        n_blocks = (i * TQ + TQ - 1) // TK + 1
        for j in range(n_blocks):
            qi.append(i)
            kj.append(j)
            first.append(int(j == 0))
            last.append(int(j == n_blocks - 1))
            kind.append(0 if j < n_blocks - 1 else 1 + (i * TQ - j * TK) // TQ)
    pad = [0] * (2 * ATT_UNROLL)
    tables = [np.asarray(pad + t + pad, np.int32) for t in (qi, kj, first, last, kind)]
    return len(qi), tables


def _attn_mask_tiles():
    kk = np.arange(TK)[:, None]
    qq = np.arange(TQ)[None, :]
    tiles = [np.zeros((TK, TQ), np.float32)]
    for d in range(TK // TQ):
        tiles.append(np.where(kk <= qq + d * TQ, 0.0, MASK_VALUE).astype(np.float32))
    return np.stack(tiles)


def _attn_kernel(qi_ref, kj_ref, first_ref, last_ref, kind_ref,
                 q_ref, k_ref, vt_ref, mask_ref, o_ref, s_buf, p_buf, *, n_iter):
    U = ATT_UNROLL
    s_buf[...] = jnp.zeros(s_buf.shape, _f32)
    p_buf[...] = jnp.zeros(p_buf.shape, _bf16)

    def head_cols(ref, hd, width, rows):
        return ref[rows, hd * width:(hd + 1) * width]

    def stage_a(hd, e, slot):
        q0 = pl.multiple_of(qi_ref[e] * TQ, TQ)
        k0 = pl.multiple_of(kj_ref[e] * TK, TK)
        s = lax.dot_general(head_cols(k_ref, hd, HEAD_SLOT, pl.ds(k0, TK)),
                            head_cols(q_ref, hd, HEAD_SLOT, pl.ds(q0, TQ)), _NT,
                            preferred_element_type=_f32)
        s = s + mask_ref[kind_ref[e]]
        s_buf[hd, slot] = s
        return jnp.max(s, axis=0, keepdims=True)

    def stage_b(hd, e, slot, m, cmax):
        m_prev = jnp.where(first_ref[e] != 0, -jnp.inf, m)
        m_new = jnp.maximum(m_prev, cmax)
        p_buf[hd, slot] = jnp.exp2(s_buf[hd, slot] - m_new).astype(_bf16)
        return m_new, jnp.exp2(m_prev - m_new)

    ones_rows = jnp.ones((SUM_ROWS, TK), _bf16)

    def stage_c(hd, e, slot, acc, l, alpha):
        k0 = pl.multiple_of(kj_ref[e] * TK, TK)
        vt = vt_ref[hd * V_HEAD_DIM:(hd + 1) * V_HEAD_DIM, pl.ds(k0, TK)]
        pv = _dot(jnp.concatenate([vt, ones_rows], axis=0), p_buf[hd, slot])
        return alpha * acc + pv[:V_HEAD_DIM], alpha * l + pv[V_HEAD_DIM:V_HEAD_DIM + 1]

    def iteration(it, parity, carry):
        state = [dict(c) for c in carry]
        alpha_new = [list(st["alpha"]) for st in state]
        for u in range(U):
            for hd in range(ATT_HEADS):
                st = state[hd]
                st["m"], alpha_new[hd][u] = stage_b(hd, (it + 1) * U + u, (1 - parity) * U + u,
                                                    st["m"], st["cmax"][u])
        cmax_new = [[None] * U for _ in range(ATT_HEADS)]
        for u in range(U):
            for hd in range(ATT_HEADS):
                cmax_new[hd][u] = stage_a(hd, (it + 2) * U + u, parity * U + u)
        done = []
        for u in range(U):
            for hd in range(ATT_HEADS):
                st = state[hd]
                e = it * U + u
                st["acc"], st["l"] = stage_c(hd, e, parity * U + u, st["acc"], st["l"],
                                             st["alpha"][u])
                done.append((hd, e, st["acc"], st["l"]))
        for hd in range(ATT_HEADS):
            state[hd]["alpha"] = tuple(alpha_new[hd])
            state[hd]["cmax"] = tuple(cmax_new[hd])
        for hd, e, acc, l in done:
            @pl.when(last_ref[e] != 0)
            def _():
                q0 = pl.multiple_of(qi_ref[e] * TQ, TQ)
                o_ref[pl.ds(q0, TQ), hd * V_HEAD_DIM:(hd + 1) * V_HEAD_DIM] = (
                    (acc / l).T.astype(_bf16))
        return tuple(state)

    def two_iterations(i, carry):
        return iteration(2 * i + 1, 1, iteration(2 * i, 0, carry))

    zeros = jnp.zeros((1, TQ), _f32)
    init = dict(m=zeros, l=zeros, acc=jnp.zeros((V_HEAD_DIM, TQ), _f32),
                alpha=(zeros,) * U, cmax=(zeros,) * U)
    lax.fori_loop(0, n_iter // 2, two_iterations, (init,) * ATT_HEADS)


def _post_kernel(attn_ref, sg_ref, gc_ref, x_ref, w_o_ref, w_out_ref, g_mix_post_ref,
                 g_mlp_pre_ref, w_ff1_ref, w_ff2_ref, g_mlp_post_ref, out_ref):
    tm = x_ref.shape[0]
    halves = [pl.ds(i * (tm // POST_SPLIT), tm // POST_SPLIT) for i in range(POST_SPLIT)]

    merged = []
    for rows in halves:
        y_attn = _dot(attn_ref[rows, :], w_o_ref[...])
        merged.append((sg_ref[rows, :].astype(_f32) * y_attn
                       + gc_ref[rows, :].astype(_f32)).astype(_bf16))
    x1, h2 = [], []
    for rows, mg in zip(halves, merged):
        x1.append(x_ref[rows, :] + _rms(_dot(mg, w_out_ref[...]), g_mix_post_ref[...]))
        h2.append(_rms(x1[-1], g_mlp_pre_ref[...]).astype(_bf16))
    f = [jnp.zeros(x.shape, _f32) for x in x1]
    for c in range(D_FF // FF_CHUNK):
        cols = slice(c * FF_CHUNK, (c + 1) * FF_CHUNK)
        for i in range(POST_SPLIT):
            hc = jnp.maximum(_dot(h2[i], w_ff1_ref[:, cols]), 0.0)
            f[i] = f[i] + _dot((hc * hc).astype(_bf16), w_ff2_ref[cols, :])
    for rows, xi, fi in zip(halves, x1, f):
        out_ref[rows, :] = xi + _rms(fi, g_mlp_post_ref[...])


def _resident(shape):
    return pl.BlockSpec(shape, lambda *_: (0,) * len(shape), pipeline_mode=pl.Buffered(1))


def _rows(tm, cols):
    return pl.BlockSpec((tm, cols), lambda i: (i, 0))


def kernel(x, positions, norm_mix_pre, w_in, q_norm, w_uq, kv_norm, w_uk, w_uv, w_o_attn,
           conv_w, conv_b, conv_ln_g, conv_ln_b, w_pw2, b_pw2, w_out, norm_mix_post,
           norm_mlp_pre, w_ff1, w_ff2, norm_mlp_post):
    B, S, D = x.shape
    T = B * S
    assert D == D_MODEL and w_in.shape[0] == 1
    assert S % TM_IN == 0 and S % TK == 0 and TK % TQ == 0 and T % TM_POST == 0
    assert N_HEADS % ATT_HEADS == 0
    C = CONV_CHANNELS
    H = N_HEADS
    tiles_per_seq = S // TM_IN

    w_in_t = jnp.swapaxes(w_in[0], 0, 1).astype(_bf16)
    o_kv, o_kr = Q_LORA_RANK + KV_LORA_RANK, Q_LORA_RANK + KV_LORA_RANK + QK_ROPE_DIM
    k1 = w_in_t[o_kv:o_kv + HALF_ROPE]
    k2 = w_in_t[o_kv + HALF_ROPE:o_kr]
    w_small = jnp.concatenate([w_in_t[:o_kv], k1, k2, k2, k1], axis=0)
    w_big = w_in_t[o_kr:]

    wq = w_uq[0].reshape(Q_LORA_RANK, H, QK_NOPE_DIM + QK_ROPE_DIM)
    r1 = wq[:, :, QK_NOPE_DIM:QK_NOPE_DIM + HALF_ROPE]
    r2 = wq[:, :, QK_NOPE_DIM + HALF_ROPE:]
    w_uq_k = jnp.concatenate([wq[:, :, :QK_NOPE_DIM], r1, r2, r2, r1], axis=2)
    w_uq_k = w_uq_k.reshape(Q_LORA_RANK, H * HEAD_SLOT).astype(_bf16)
    w_uk_k = w_uk[0].reshape(KV_LORA_RANK, H * QK_NOPE_DIM).astype(_bf16)
    w_uvt_k = w_uv[0].reshape(KV_LORA_RANK, H * V_HEAD_DIM).T.astype(_bf16)
    cw = jnp.pad(conv_w[0].reshape(CONV_WIDTH, C), ((0, 1), (0, 0)))
    cw = cw.reshape(CONV_WIDTH + 1, C // LANES, LANES).transpose(1, 0, 2)
    cb = conv_b[0].reshape(C // LANES, 1, LANES)

    inv_freq = ROPE_THETA ** (-jnp.arange(0, QK_ROPE_DIM, 2, dtype=_f32) / QK_ROPE_DIM)
    invf4 = jnp.tile(inv_freq, 4).reshape(1, ROPE_LANES)

    row = lambda a: a.reshape(1, -1)
    x2 = x.reshape(T, D)
    pos2 = positions.reshape(T // TM_IN, ROPE_PACK, TM_IN // ROPE_PACK).transpose(0, 2, 1)

    q, k, vt, sg, gc = pl.pallas_call(
        functools.partial(_inproj_kernel, tiles_per_seq=tiles_per_seq),
        grid=(T // TM_IN,),
        in_specs=[
            _rows(TM_IN, D),
            pl.BlockSpec((None, TM_IN // ROPE_PACK, ROPE_PACK), lambda i: (i, 0, 0)),
            _resident((1, ROPE_LANES)), _resident((1, D)),
            _resident(w_small.shape), _resident(w_big.shape),
            _resident((1, Q_LORA_RANK)), _resident(w_uq_k.shape),
            _resident((1, KV_LORA_RANK)), _resident(w_uk_k.shape), _resident(w_uvt_k.shape),
            _resident(cw.shape), _resident(cb.shape), _resident((1, C)), _resident((1, C)),
            _resident((C, D)), _resident((1, D)),
        ],
        out_specs=[_rows(TM_IN, H * HEAD_SLOT), _rows(TM_IN, H * HEAD_SLOT),
                   pl.BlockSpec((None, H * V_HEAD_DIM, TM_IN),
                                lambda i: (i // tiles_per_seq, 0, i % tiles_per_seq)),
                   _rows(TM_IN, D), _rows(TM_IN, D)],
        out_shape=[jax.ShapeDtypeStruct((T, H * HEAD_SLOT), _bf16),
                   jax.ShapeDtypeStruct((T, H * HEAD_SLOT), _bf16),
                   jax.ShapeDtypeStruct((B, H * V_HEAD_DIM, S), _bf16),
                   jax.ShapeDtypeStruct((T, D), _bf16),
                   jax.ShapeDtypeStruct((T, D), _bf16)],
        scratch_shapes=[pltpu.VMEM((C // LANES, HALO + TM_IN, LANES), _f32),
                        pltpu.VMEM((C // LANES, TM_IN, LANES), _f32),
                        pltpu.VMEM((TM_IN, D), _bf16)],
        compiler_params=pltpu.CompilerParams(dimension_semantics=("arbitrary",),
                                             vmem_limit_bytes=VMEM_LIMIT),
        name="inproj",
    )(x2, pos2, invf4, row(norm_mix_pre[0]), w_small, w_big, row(q_norm[0]), w_uq_k, row(kv_norm[0]),
      w_uk_k, w_uvt_k, cw, cb, row(conv_ln_g[0]), row(conv_ln_b[0]),
      w_pw2[0].astype(_bf16), row(b_pw2[0]))

    n_blocks, tables = _attn_stream(S)
    assert n_blocks % (2 * ATT_UNROLL) == 0
    n_iter = n_blocks // ATT_UNROLL + 2
    mask_tiles = _attn_mask_tiles()
    qk_spec = pl.BlockSpec((S, ATT_HEADS * HEAD_SLOT), lambda b, g, *_: (b, g))
    attn = pl.pallas_call(
        functools.partial(_attn_kernel, n_iter=n_iter),
        grid_spec=pltpu.PrefetchScalarGridSpec(
            num_scalar_prefetch=len(tables),
            grid=(B, H // ATT_HEADS),
            in_specs=[qk_spec, qk_spec,
                      pl.BlockSpec((None, ATT_HEADS * V_HEAD_DIM, S), lambda b, g, *_: (b, g, 0)),
                      pl.BlockSpec(mask_tiles.shape, lambda b, g, *_: (0, 0, 0))],
            out_specs=pl.BlockSpec((S, ATT_HEADS * V_HEAD_DIM), lambda b, g, *_: (b, g)),
            scratch_shapes=[pltpu.VMEM((ATT_HEADS, 2 * ATT_UNROLL, TK, TQ), _f32),
                            pltpu.VMEM((ATT_HEADS, 2 * ATT_UNROLL, TK, TQ), _bf16)]),
        out_shape=jax.ShapeDtypeStruct((T, H * V_HEAD_DIM), _bf16),
        compiler_params=pltpu.CompilerParams(dimension_semantics=("arbitrary", "arbitrary"),
                                             vmem_limit_bytes=VMEM_LIMIT),
        name="attn",
    )(*[jnp.asarray(t) for t in tables], q, k, vt, jnp.asarray(mask_tiles))

    out = pl.pallas_call(
        _post_kernel,
        grid=(T // TM_POST,),
        in_specs=[_rows(TM_POST, H * V_HEAD_DIM), _rows(TM_POST, D), _rows(TM_POST, D),
                  _rows(TM_POST, D), _resident((H * V_HEAD_DIM, D)), _resident((D, D)),
                  _resident((1, D)), _resident((1, D)), _resident((D, D_FF)),
                  _resident((D_FF, D)), _resident((1, D))],
        out_specs=_rows(TM_POST, D),
        out_shape=jax.ShapeDtypeStruct((T, D), _f32),
        compiler_params=pltpu.CompilerParams(dimension_semantics=("arbitrary",),
                                             vmem_limit_bytes=VMEM_LIMIT),
        name="post",
    )(attn, sg, gc, x2, w_o_attn[0].astype(_bf16), w_out[0].astype(_bf16),
      row(norm_mix_post[0]), row(norm_mlp_pre[0]), w_ff1[0].astype(_bf16),
      w_ff2[0].astype(_bf16), row(norm_mlp_post[0]))
    return out.reshape(B, S, D)
```

```python
import functools
import math

import jax
import jax.numpy as jnp
import numpy as np
from jax import lax
from jax.experimental import pallas as pl
from jax.experimental.pallas import tpu as pltpu

D_MODEL = 1024
N_HEADS = 8
QK_NOPE_DIM = 128
QK_ROPE_DIM = 64
V_HEAD_DIM = 128
Q_LORA_RANK = 384
KV_LORA_RANK = 256
ROPE_THETA = 10000.0
CONV_CHANNELS = 1024
CONV_WIDTH = 31
D_FF = 4096
EPS = 1e-6

HALF_ROPE = QK_ROPE_DIM // 2
ROPE_PACK = 4
ROPE_LANES = ROPE_PACK * HALF_ROPE
HEAD_SLOT = QK_NOPE_DIM + ROPE_LANES
SMALL_COLS = Q_LORA_RANK + KV_LORA_RANK + ROPE_LANES
HALO = 32
SUBLANES = 8
LANES = 128
CONV_ROWS = 64
CONV_STRIDE = 2

Q_SCALE = (QK_NOPE_DIM + QK_ROPE_DIM) ** -0.5 * math.log2(math.e)
MASK_VALUE = -1e30

TM_IN = 512
TM_POST = 512
TQ = 256
TK = 512
ATT_HEADS = 2
ATT_UNROLL = 2
SUM_ROWS = 16
FF_CHUNK = 1024
POST_SPLIT = 2

VMEM_LIMIT = 60 * 1024 * 1024

_bf16 = jnp.bfloat16
_f32 = jnp.float32
_NT = (((1,), (1,)), ((), ()))


def _dot(a, b):
    return jnp.dot(a, b, preferred_element_type=_f32)


def _dot_nt(a, b):
    return lax.dot_general(a, b, _NT, preferred_element_type=_f32)


def _rms(x, g):
    return x * lax.rsqrt(jnp.mean(x * x, axis=-1, keepdims=True) + EPS) * g


def _inproj_kernel(x_ref, pos_ref, invf_ref, g_pre_ref, w_small_ref, w_big_ref, qn_g_ref, w_uq_ref,
                   kvn_g_ref, w_uk_ref, w_uvt_ref, cw_ref, cb_ref, lng_ref, lnb_ref,
                   w_pw2_ref, b_pw2_ref,
                   q_ref, k_ref, vt_ref, sg_ref, gc_ref,
                   ubuf, cacc, hbuf, *, tiles_per_seq):
    tm = x_ref.shape[0]
    C = CONV_CHANNELS

    n_slabs = C // LANES

    @pl.when(pl.program_id(0) % tiles_per_seq == 0)
    def _():
        ubuf[:, 0:HALO, :] = jnp.zeros((n_slabs, HALO, LANES), _f32)

    hbuf[...] = _rms(x_ref[...], g_pre_ref[...]).astype(_bf16)
    a = _dot_nt(hbuf[...], w_big_ref[0:C, :])
    b = _dot_nt(hbuf[...], w_big_ref[C:2 * C, :])
    u = a * jax.nn.sigmoid(b)
    for c in range(n_slabs):
        ubuf[c, HALO:HALO + tm, :] = u[:, c * LANES:(c + 1) * LANES]

    shift = HALO - (CONV_WIDTH - 1)
    group = CONV_STRIDE * SUBLANES

    def conv_slab(c, carry):
        for r0 in range(0, tm, CONV_ROWS):
            starts = [r0 + g * group + p for g in range(CONV_ROWS // group)
                      for p in range(CONV_STRIDE)]
            accs = [jnp.broadcast_to(cb_ref[c], (SUBLANES, LANES)) for _ in starts]
            for t in range(CONV_WIDTH):
                w = cw_ref[c, pl.ds(t, 1), :]
                for j, start in enumerate(starts):
                    rows = pl.ds(start + shift + t, SUBLANES, stride=CONV_STRIDE)
                    accs[j] = accs[j] + ubuf[c, rows, :] * w
            for acc, start in zip(accs, starts):
                cacc[c, pl.ds(start, SUBLANES, stride=CONV_STRIDE), :] = acc
        return carry

    lax.fori_loop(0, n_slabs, conv_slab, 0)
    ubuf[:, 0:HALO, :] = ubuf[:, tm:tm + HALO, :]

    conv = jnp.concatenate([cacc[c] for c in range(n_slabs)], axis=1)
    xc = conv - jnp.mean(conv, axis=-1, keepdims=True)
    var = jnp.mean(xc * xc, axis=-1, keepdims=True)
    y = xc * lax.rsqrt(var + EPS) * lng_ref[...] + lnb_ref[...]
    swish = (y * jax.nn.sigmoid(y)).astype(_bf16)

    h = hbuf[...]
    sg_ref[...] = jax.nn.sigmoid(_dot_nt(h, w_big_ref[2 * C:3 * C, :])).astype(_bf16)
    g_conv = jax.nn.sigmoid(_dot_nt(h, w_big_ref[3 * C:4 * C, :]))

    zs = _dot_nt(h, w_small_ref[...])
    cq = zs[:, 0:Q_LORA_RANK]
    ckv = zs[:, Q_LORA_RANK:Q_LORA_RANK + KV_LORA_RANK]
    kr4 = zs[:, Q_LORA_RANK + KV_LORA_RANK:SMALL_COLS]

    pos = pos_ref[...].astype(_f32)
    lane = lax.broadcasted_iota(jnp.int32, (tm // ROPE_PACK, ROPE_LANES), 1)
    in_group = [lane < (g + 1) * HALF_ROPE for g in range(ROPE_PACK - 1)]

    def by_group(parts):
        out = parts[-1]
        for g in reversed(range(ROPE_PACK - 1)):
            out = jnp.where(in_group[g], parts[g], out)
        return out

    ang = by_group([pos[:, g:g + 1] for g in range(ROPE_PACK)]) * invf_ref[...]
    cos, sin = jnp.cos(ang), jnp.sin(ang)
    cos_rot = [cos] + [pltpu.roll(cos, g * HALF_ROPE, 1) for g in range(1, ROPE_PACK)]
    sin_rot = [sin] + [pltpu.roll(sin, g * HALF_ROPE, 1) for g in range(1, ROPE_PACK)]
    rope4 = jnp.concatenate(
        [by_group([cos_rot[(0 - g) % ROPE_PACK], cos_rot[(1 - g) % ROPE_PACK],
                   -sin_rot[(2 - g) % ROPE_PACK], sin_rot[(3 - g) % ROPE_PACK]])
         for g in range(ROPE_PACK)], axis=0)

    q = _dot(_rms(cq, qn_g_ref[...]).astype(_bf16), w_uq_ref[...])
    rope4_q = rope4 * Q_SCALE
    for hd in range(N_HEADS):
        lo = hd * HEAD_SLOT
        q_ref[:, lo:lo + QK_NOPE_DIM] = (q[:, lo:lo + QK_NOPE_DIM] * Q_SCALE).astype(_bf16)
        q_ref[:, lo + QK_NOPE_DIM:lo + HEAD_SLOT] = (
            q[:, lo + QK_NOPE_DIM:lo + HEAD_SLOT] * rope4_q).astype(_bf16)

    kp = kr4 * rope4
    kdup = (kp + pltpu.roll(kp, 2 * HALF_ROPE, 1)).astype(_bf16)
    ckvn = _rms(ckv, kvn_g_ref[...]).astype(_bf16)
    kn = _dot(ckvn, w_uk_ref[...])
    for hd in range(N_HEADS):
        lo = hd * HEAD_SLOT
        k_ref[:, lo:lo + QK_NOPE_DIM] = kn[:, hd * QK_NOPE_DIM:(hd + 1) * QK_NOPE_DIM].astype(_bf16)
        k_ref[:, lo + QK_NOPE_DIM:lo + HEAD_SLOT] = kdup
    vt_ref[...] = _dot_nt(w_uvt_ref[...], ckvn).astype(_bf16)

    y_conv = _dot(swish, w_pw2_ref[...]) + b_pw2_ref[...]
    gc_ref[...] = (g_conv * y_conv).astype(_bf16)


def _attn_stream(seq):
    qi, kj, first, last, kind = [], [], [], [], []
    for i in range(seq // TQ):
        n_blocks = (i * TQ + TQ - 1) // TK + 1
        for j in range(n_blocks):
            qi.append(i)
            kj.append(j)
            first.append(int(j == 0))
            last.append(int(j == n_blocks - 1))
            kind.append(0 if j < n_blocks - 1 else 1 + (i * TQ - j * TK) // TQ)
    pad = [0] * (2 * ATT_UNROLL)
    tables = [np.asarray(pad + t + pad, np.int32) for t in (qi, kj, first, last, kind)]
    return len(qi), tables


def _attn_mask_tiles():
    kk = np.arange(TK)[:, None]
    qq = np.arange(TQ)[None, :]
    tiles = [np.zeros((TK, TQ), np.float32)]
    for d in range(TK // TQ):
        tiles.append(np.where(kk <= qq + d * TQ, 0.0, MASK_VALUE).astype(np.float32))
    return np.stack(tiles)


def _attn_kernel(qi_ref, kj_ref, first_ref, last_ref, kind_ref,
                 q_ref, k_ref, vt_ref, mask_ref, o_ref, s_buf, p_buf, *, n_iter):
    U = ATT_UNROLL
    s_buf[...] = jnp.zeros(s_buf.shape, _f32)
    p_buf[...] = jnp.zeros(p_buf.shape, _bf16)

    def head_cols(ref, hd, width, rows):
        return ref[rows, hd * width:(hd + 1) * width]

    def stage_a(hd, e, slot):
        q0 = pl.multiple_of(qi_ref[e] * TQ, TQ)
        k0 = pl.multiple_of(kj_ref[e] * TK, TK)
        s = lax.dot_general(head_cols(k_ref, hd, HEAD_SLOT, pl.ds(k0, TK)),
                            head_cols(q_ref, hd, HEAD_SLOT, pl.ds(q0, TQ)), _NT,
                            preferred_element_type=_f32)
        s = s + mask_ref[kind_ref[e]]
        s_buf[hd, slot] = s
        return jnp.max(s, axis=0, keepdims=True)

    def stage_b(hd, e, slot, m, cmax):
        m_prev = jnp.where(first_ref[e] != 0, -jnp.inf, m)
        m_new = jnp.maximum(m_prev, cmax)
        p_buf[hd, slot] = jnp.exp2(s_buf[hd, slot] - m_new).astype(_bf16)
        return m_new, jnp.exp2(m_prev - m_new)

    ones_rows = jnp.ones((SUM_ROWS, TK), _bf16)

    def stage_c(hd, e, slot, acc, l, alpha):
        k0 = pl.multiple_of(kj_ref[e] * TK, TK)
        vt = vt_ref[hd * V_HEAD_DIM:(hd + 1) * V_HEAD_DIM, pl.ds(k0, TK)]
        pv = _dot(jnp.concatenate([vt, ones_rows], axis=0), p_buf[hd, slot])
        return alpha * acc + pv[:V_HEAD_DIM], alpha * l + pv[V_HEAD_DIM:V_HEAD_DIM + 1]

    def iteration(it, parity, carry):
        state = [dict(c) for c in carry]
        alpha_new = [list(st["alpha"]) for st in state]
        for u in range(U):
            for hd in range(ATT_HEADS):
                st = state[hd]
                st["m"], alpha_new[hd][u] = stage_b(hd, (it + 1) * U + u, (1 - parity) * U + u,
                                                    st["m"], st["cmax"][u])
        cmax_new = [[None] * U for _ in range(ATT_HEADS)]
        for u in range(U):
            for hd in range(ATT_HEADS):
                cmax_new[hd][u] = stage_a(hd, (it + 2) * U + u, parity * U + u)
        done = []
        for u in range(U):
            e = it * U + u
            for hd in range(ATT_HEADS):
                st = state[hd]
                st["acc"], st["l"] = stage_c(hd, e, parity * U + u, st["acc"], st["l"],
                                             st["alpha"][u])
            done.append((e, [(st["acc"], st["l"]) for st in state]))
        for hd in range(ATT_HEADS):
            state[hd]["alpha"] = tuple(alpha_new[hd])
            state[hd]["cmax"] = tuple(cmax_new[hd])
        for e, heads in done:
            @pl.when(last_ref[e] != 0)
            def _():
                q0 = pl.multiple_of(qi_ref[e] * TQ, TQ)
                for hd, (acc, l) in enumerate(heads):
                    o_ref[pl.ds(q0, TQ), hd * V_HEAD_DIM:(hd + 1) * V_HEAD_DIM] = (
                        (acc / l).T.astype(_bf16))
        return tuple(state)

    def two_iterations(i, carry):
        return iteration(2 * i + 1, 1, iteration(2 * i, 0, carry))

    zeros = jnp.zeros((1, TQ), _f32)
    init = dict(m=zeros, l=zeros, acc=jnp.zeros((V_HEAD_DIM, TQ), _f32),
                alpha=(zeros,) * U, cmax=(zeros,) * U)
    lax.fori_loop(0, n_iter // 2, two_iterations, (init,) * ATT_HEADS)


def _post_kernel(attn_ref, sg_ref, gc_ref, x_ref, w_o_ref, w_out_ref, g_mix_post_ref,
                 g_mlp_pre_ref, w_ff1_ref, w_ff2_ref, g_mlp_post_ref, out_ref):
    tm = x_ref.shape[0]
    halves = [pl.ds(i * (tm // POST_SPLIT), tm // POST_SPLIT) for i in range(POST_SPLIT)]

    merged = []
    for rows in halves:
        y_attn = _dot(attn_ref[rows, :], w_o_ref[...])
        merged.append((sg_ref[rows, :].astype(_f32) * y_attn
                       + gc_ref[rows, :].astype(_f32)).astype(_bf16))
    x1, h2 = [], []
    for rows, mg in zip(halves, merged):
        x1.append(x_ref[rows, :] + _rms(_dot(mg, w_out_ref[...]), g_mix_post_ref[...]))
        h2.append(_rms(x1[-1], g_mlp_pre_ref[...]).astype(_bf16))
    f = [jnp.zeros(x.shape, _f32) for x in x1]
    for c in range(D_FF // FF_CHUNK):
        cols = slice(c * FF_CHUNK, (c + 1) * FF_CHUNK)
        for i in range(POST_SPLIT):
            hc = jnp.maximum(_dot(h2[i], w_ff1_ref[:, cols]), 0.0)
            f[i] = f[i] + _dot((hc * hc).astype(_bf16), w_ff2_ref[cols, :])
    for rows, xi, fi in zip(halves, x1, f):
        out_ref[rows, :] = xi + _rms(fi, g_mlp_post_ref[...])


def _resident(shape):
    return pl.BlockSpec(shape, lambda *_: (0,) * len(shape), pipeline_mode=pl.Buffered(1))


def _rows(tm, cols):
    return pl.BlockSpec((tm, cols), lambda i: (i, 0))


def kernel(x, positions, norm_mix_pre, w_in, q_norm, w_uq, kv_norm, w_uk, w_uv, w_o_attn,
           conv_w, conv_b, conv_ln_g, conv_ln_b, w_pw2, b_pw2, w_out, norm_mix_post,
           norm_mlp_pre, w_ff1, w_ff2, norm_mlp_post):
    B, S, D = x.shape
    T = B * S
    assert D == D_MODEL and w_in.shape[0] == 1
    assert S % TM_IN == 0 and S % TK == 0 and TK % TQ == 0 and T % TM_POST == 0
    assert N_HEADS % ATT_HEADS == 0
    C = CONV_CHANNELS
    H = N_HEADS
    tiles_per_seq = S // TM_IN

    w_in_t = jnp.swapaxes(w_in[0], 0, 1).astype(_bf16)
    o_kv, o_kr = Q_LORA_RANK + KV_LORA_RANK, Q_LORA_RANK + KV_LORA_RANK + QK_ROPE_DIM
    k1 = w_in_t[o_kv:o_kv + HALF_ROPE]
    k2 = w_in_t[o_kv + HALF_ROPE:o_kr]
    w_small = jnp.concatenate([w_in_t[:o_kv], k1, k2, k2, k1], axis=0)
    w_big = w_in_t[o_kr:]

    wq = w_uq[0].reshape(Q_LORA_RANK, H, QK_NOPE_DIM + QK_ROPE_DIM)
    r1 = wq[:, :, QK_NOPE_DIM:QK_NOPE_DIM + HALF_ROPE]
    r2 = wq[:, :, QK_NOPE_DIM + HALF_ROPE:]
    w_uq_k = jnp.concatenate([wq[:, :, :QK_NOPE_DIM], r1, r2, r2, r1], axis=2)
    w_uq_k = w_uq_k.reshape(Q_LORA_RANK, H * HEAD_SLOT).astype(_bf16)
    w_uk_k = w_uk[0].reshape(KV_LORA_RANK, H * QK_NOPE_DIM).astype(_bf16)
    w_uvt_k = w_uv[0].reshape(KV_LORA_RANK, H * V_HEAD_DIM).T.astype(_bf16)
    cw = jnp.pad(conv_w[0].reshape(CONV_WIDTH, C), ((0, 1), (0, 0)))
    cw = cw.reshape(CONV_WIDTH + 1, C // LANES, LANES).transpose(1, 0, 2)
    cb = conv_b[0].reshape(C // LANES, 1, LANES)

    inv_freq = ROPE_THETA ** (-jnp.arange(0, QK_ROPE_DIM, 2, dtype=_f32) / QK_ROPE_DIM)
    invf4 = jnp.tile(inv_freq, 4).reshape(1, ROPE_LANES)

    row = lambda a: a.reshape(1, -1)
    x2 = x.reshape(T, D)
    pos2 = positions.reshape(T // TM_IN, ROPE_PACK, TM_IN // ROPE_PACK).transpose(0, 2, 1)

    q, k, vt, sg, gc = pl.pallas_call(
        functools.partial(_inproj_kernel, tiles_per_seq=tiles_per_seq),
        grid=(T // TM_IN,),
        in_specs=[
            _rows(TM_IN, D),
            pl.BlockSpec((None, TM_IN // ROPE_PACK, ROPE_PACK), lambda i: (i, 0, 0)),
            _resident((1, ROPE_LANES)), _resident((1, D)),
            _resident(w_small.shape), _resident(w_big.shape),
            _resident((1, Q_LORA_RANK)), _resident(w_uq_k.shape),
            _resident((1, KV_LORA_RANK)), _resident(w_uk_k.shape), _resident(w_uvt_k.shape),
            _resident(cw.shape), _resident(cb.shape), _resident((1, C)), _resident((1, C)),
            _resident((C, D)), _resident((1, D)),
        ],
        out_specs=[_rows(TM_IN, H * HEAD_SLOT), _rows(TM_IN, H * HEAD_SLOT),
                   pl.BlockSpec((None, H * V_HEAD_DIM, TM_IN),
                                lambda i: (i // tiles_per_seq, 0, i % tiles_per_seq)),
                   _rows(TM_IN, D), _rows(TM_IN, D)],
        out_shape=[jax.ShapeDtypeStruct((T, H * HEAD_SLOT), _bf16),
                   jax.ShapeDtypeStruct((T, H * HEAD_SLOT), _bf16),
                   jax.ShapeDtypeStruct((B, H * V_HEAD_DIM, S), _bf16),
                   jax.ShapeDtypeStruct((T, D), _bf16),
                   jax.ShapeDtypeStruct((T, D), _bf16)],
        scratch_shapes=[pltpu.VMEM((C // LANES, HALO + TM_IN, LANES), _f32),
                        pltpu.VMEM((C // LANES, TM_IN, LANES), _f32),
                        pltpu.VMEM((TM_IN, D), _bf16)],
        compiler_params=pltpu.CompilerParams(dimension_semantics=("arbitrary",),
                                             vmem_limit_bytes=VMEM_LIMIT),
        name="inproj",
    )(x2, pos2, invf4, row(norm_mix_pre[0]), w_small, w_big, row(q_norm[0]), w_uq_k, row(kv_norm[0]),
      w_uk_k, w_uvt_k, cw, cb, row(conv_ln_g[0]), row(conv_ln_b[0]),
      w_pw2[0].astype(_bf16), row(b_pw2[0]))

    n_blocks, tables = _attn_stream(S)
    assert n_blocks % (2 * ATT_UNROLL) == 0
    n_iter = n_blocks // ATT_UNROLL + 2
    mask_tiles = _attn_mask_tiles()
    qk_spec = pl.BlockSpec((S, ATT_HEADS * HEAD_SLOT), lambda b, g, *_: (b, g))
    attn = pl.pallas_call(
        functools.partial(_attn_kernel, n_iter=n_iter),
        grid_spec=pltpu.PrefetchScalarGridSpec(
            num_scalar_prefetch=len(tables),
            grid=(B, H // ATT_HEADS),
            in_specs=[qk_spec, qk_spec,
                      pl.BlockSpec((None, ATT_HEADS * V_HEAD_DIM, S), lambda b, g, *_: (b, g, 0)),
                      pl.BlockSpec(mask_tiles.shape, lambda b, g, *_: (0, 0, 0))],
            out_specs=pl.BlockSpec((S, ATT_HEADS * V_HEAD_DIM), lambda b, g, *_: (b, g)),
            scratch_shapes=[pltpu.VMEM((ATT_HEADS, 2 * ATT_UNROLL, TK, TQ), _f32),
                            pltpu.VMEM((ATT_HEADS, 2 * ATT_UNROLL, TK, TQ), _bf16)]),
        out_shape=jax.ShapeDtypeStruct((T, H * V_HEAD_DIM), _bf16),
        compiler_params=pltpu.CompilerParams(dimension_semantics=("arbitrary", "arbitrary"),
                                             vmem_limit_bytes=VMEM_LIMIT),
        name="attn",
    )(*[jnp.asarray(t) for t in tables], q, k, vt, jnp.asarray(mask_tiles))

    out = pl.pallas_call(
        _post_kernel,
        grid=(T // TM_POST,),
        in_specs=[_rows(TM_POST, H * V_HEAD_DIM), _rows(TM_POST, D), _rows(TM_POST, D),
                  _rows(TM_POST, D), _resident((H * V_HEAD_DIM, D)), _resident((D, D)),
                  _resident((1, D)), _resident((1, D)), _resident((D, D_FF)),
                  _resident((D_FF, D)), _resident((1, D))],
        out_specs=_rows(TM_POST, D),
        out_shape=jax.ShapeDtypeStruct((T, D), _f32),
        compiler_params=pltpu.CompilerParams(dimension_semantics=("arbitrary",),
                                             vmem_limit_bytes=VMEM_LIMIT),
        name="post",
    )(attn, sg, gc, x2, w_o_attn[0].astype(_bf16), w_out[0].astype(_bf16),
      row(norm_mix_post[0]), row(norm_mlp_pre[0]), w_ff1[0].astype(_bf16),
      w_ff2[0].astype(_bf16), row(norm_mlp_post[0]))
    return out.reshape(B, S, D)
```

```python
import functools
import math

import jax
import jax.numpy as jnp
import numpy as np
from jax import lax
from jax.experimental import pallas as pl
from jax.experimental.pallas import tpu as pltpu

D_MODEL = 1024
N_HEADS = 8
QK_NOPE_DIM = 128
QK_ROPE_DIM = 64
V_HEAD_DIM = 128
Q_LORA_RANK = 384
KV_LORA_RANK = 256
ROPE_THETA = 10000.0
CONV_CHANNELS = 1024
CONV_WIDTH = 31
D_FF = 4096
EPS = 1e-6

HALF_ROPE = QK_ROPE_DIM // 2
ROPE_PACK = 4
ROPE_LANES = ROPE_PACK * HALF_ROPE
HEAD_SLOT = QK_NOPE_DIM + ROPE_LANES
SMALL_COLS = Q_LORA_RANK + KV_LORA_RANK + ROPE_LANES
HALO = 32
SUBLANES = 8
LANES = 128
CONV_ROWS = 64
CONV_STRIDE = 2

Q_SCALE = (QK_NOPE_DIM + QK_ROPE_DIM) ** -0.5 * math.log2(math.e)
MASK_VALUE = -1e30

TM_IN = 512
TM_POST = 512
TQ = 256
TK = 512
ATT_HEADS = 2
ATT_UNROLL = 2
SUM_ROWS = 16
FF_CHUNK = 1024
POST_SPLIT = 2

V7X_VMEM_BYTES = 64 * 1024 * 1024
VMEM_LIMIT = V7X_VMEM_BYTES - 4 * 1024 * 1024

_bf16 = jnp.bfloat16
_f32 = jnp.float32
_NT = (((1,), (1,)), ((), ()))
_TN = (((0,), (0,)), ((), ()))


def _dot(a, b):
    return jnp.dot(a, b, preferred_element_type=_f32)


def _dot_nt(a, b):
    return lax.dot_general(a, b, _NT, preferred_element_type=_f32)


def _rms(x, g):
    return x * lax.rsqrt(jnp.mean(x * x, axis=-1, keepdims=True) + EPS) * g


def _inproj_kernel(x_ref, pos_ref, invf_ref, g_pre_ref, w_small_ref, w_big_ref, qn_g_ref, w_uq_ref,
                   kvn_g_ref, w_uk_ref, w_uvt_ref, cw_ref, cb_ref, lng_ref, lnb_ref,
                   w_pw2_ref, b_pw2_ref,
                   q_ref, k_ref, vt_ref, sg_ref, gc_ref,
                   ubuf, cacc, hbuf, *, tiles_per_seq):
    tm = x_ref.shape[0]
    C = CONV_CHANNELS

    n_slabs = C // LANES

    @pl.when(pl.program_id(0) % tiles_per_seq == 0)
    def _():
        ubuf[:, 0:HALO, :] = jnp.zeros((n_slabs, HALO, LANES), _f32)

    hbuf[...] = _rms(x_ref[...], g_pre_ref[...]).astype(_bf16)
    a = _dot_nt(hbuf[...], w_big_ref[0:C, :])
    b = _dot_nt(hbuf[...], w_big_ref[C:2 * C, :])
    u = a * jax.nn.sigmoid(b)
    for c in range(n_slabs):
        ubuf[c, HALO:HALO + tm, :] = u[:, c * LANES:(c + 1) * LANES]

    shift = HALO - (CONV_WIDTH - 1)
    group = CONV_STRIDE * SUBLANES

    def conv_slab(c, carry):
        for r0 in range(0, tm, CONV_ROWS):
            starts = [r0 + g * group + p for g in range(CONV_ROWS // group)
                      for p in range(CONV_STRIDE)]
            accs = [jnp.broadcast_to(cb_ref[c], (SUBLANES, LANES)) for _ in starts]
            for t in range(CONV_WIDTH):
                w = cw_ref[c, pl.ds(t, 1), :]
                for j, start in enumerate(starts):
                    rows = pl.ds(start + shift + t, SUBLANES, stride=CONV_STRIDE)
                    accs[j] = accs[j] + ubuf[c, rows, :] * w
            for acc, start in zip(accs, starts):
                cacc[c, pl.ds(start, SUBLANES, stride=CONV_STRIDE), :] = acc
        return carry

    lax.fori_loop(0, n_slabs, conv_slab, 0)
    ubuf[:, 0:HALO, :] = ubuf[:, tm:tm + HALO, :]

    conv = jnp.concatenate([cacc[c] for c in range(n_slabs)], axis=1)
    xc = conv - jnp.mean(conv, axis=-1, keepdims=True)
    var = jnp.mean(xc * xc, axis=-1, keepdims=True)
    y = xc * lax.rsqrt(var + EPS) * lng_ref[...] + lnb_ref[...]
    swish = (y * jax.nn.sigmoid(y)).astype(_bf16)

    h = hbuf[...]
    sg_ref[...] = jax.nn.sigmoid(_dot_nt(h, w_big_ref[2 * C:3 * C, :])).astype(_bf16)
    g_conv = jax.nn.sigmoid(_dot_nt(h, w_big_ref[3 * C:4 * C, :]))

    zs = _dot_nt(h, w_small_ref[...])
    cq = zs[:, 0:Q_LORA_RANK]
    ckv = zs[:, Q_LORA_RANK:Q_LORA_RANK + KV_LORA_RANK]
    kr4 = zs[:, Q_LORA_RANK + KV_LORA_RANK:SMALL_COLS]

    pos = pos_ref[...].astype(_f32)
    lane = lax.broadcasted_iota(jnp.int32, (tm // ROPE_PACK, ROPE_LANES), 1)
    in_group = [lane < (g + 1) * HALF_ROPE for g in range(ROPE_PACK - 1)]

    def by_group(parts):
        out = parts[-1]
        for g in reversed(range(ROPE_PACK - 1)):
            out = jnp.where(in_group[g], parts[g], out)
        return out

    ang = by_group([pos[:, g:g + 1] for g in range(ROPE_PACK)]) * invf_ref[...]
    cos, sin = jnp.cos(ang), jnp.sin(ang)
    cos_rot = [cos] + [pltpu.roll(cos, g * HALF_ROPE, 1) for g in range(1, ROPE_PACK)]
    sin_rot = [sin] + [pltpu.roll(sin, g * HALF_ROPE, 1) for g in range(1, ROPE_PACK)]
    rope4 = jnp.concatenate(
        [by_group([cos_rot[(0 - g) % ROPE_PACK], cos_rot[(1 - g) % ROPE_PACK],
                   -sin_rot[(2 - g) % ROPE_PACK], sin_rot[(3 - g) % ROPE_PACK]])
         for g in range(ROPE_PACK)], axis=0)

    q = _dot(_rms(cq, qn_g_ref[...]).astype(_bf16), w_uq_ref[...])
    rope4_q = rope4 * Q_SCALE
    for hd in range(N_HEADS):
        lo = hd * HEAD_SLOT
        q_ref[:, lo:lo + QK_NOPE_DIM] = (q[:, lo:lo + QK_NOPE_DIM] * Q_SCALE).astype(_bf16)
        q_ref[:, lo + QK_NOPE_DIM:lo + HEAD_SLOT] = (
            q[:, lo + QK_NOPE_DIM:lo + HEAD_SLOT] * rope4_q).astype(_bf16)

    kp = kr4 * rope4
    kdup = (kp + pltpu.roll(kp, 2 * HALF_ROPE, 1)).astype(_bf16)
    ckvn = _rms(ckv, kvn_g_ref[...]).astype(_bf16)
    kn = _dot(ckvn, w_uk_ref[...])
    for hd in range(N_HEADS):
        lo = hd * HEAD_SLOT
        k_ref[:, lo:lo + QK_NOPE_DIM] = kn[:, hd * QK_NOPE_DIM:(hd + 1) * QK_NOPE_DIM].astype(_bf16)
        k_ref[:, lo + QK_NOPE_DIM:lo + HEAD_SLOT] = kdup
    vt_ref[...] = _dot_nt(w_uvt_ref[...], ckvn).astype(_bf16)

    y_conv = _dot(swish, w_pw2_ref[...]) + b_pw2_ref[...]
    gc_ref[...] = (g_conv * y_conv).astype(_bf16)


def _attn_stream(seq):
    qi, kj, first, last, kind = [], [], [], [], []
    for i in range(seq // TQ):
        n_blocks = (i * TQ + TQ - 1) // TK + 1
        for j in range(n_blocks):
            qi.append(i)
            kj.append(j)
            first.append(int(j == 0))
            last.append(int(j == n_blocks - 1))
            kind.append(0 if j < n_blocks - 1 else 1 + (i * TQ - j * TK) // TQ)
    pad = [0] * (2 * ATT_UNROLL)
    tables = [np.asarray(pad + t + pad, np.int32) for t in (qi, kj, first, last, kind)]
    return len(qi), tables


def _attn_mask_tiles():
    kk = np.arange(TK)[:, None]
    qq = np.arange(TQ)[None, :]
    tiles = [np.zeros((TK, TQ), np.float32)]
    for d in range(TK // TQ):
        tiles.append(np.where(kk <= qq + d * TQ, 0.0, MASK_VALUE).astype(np.float32))
    return np.stack(tiles)


def _attn_kernel(qi_ref, kj_ref, first_ref, last_ref, kind_ref,
                 q_ref, k_ref, vt_ref, mask_ref, o_ref, s_buf, p_buf, *, n_iter):
    U = ATT_UNROLL
    s_buf[...] = jnp.zeros(s_buf.shape, _f32)
    p_buf[...] = jnp.zeros(p_buf.shape, _bf16)

    def head_cols(ref, hd, width, rows):
        return ref[rows, hd * width:(hd + 1) * width]

    def stage_a(hd, e, slot):
        q0 = pl.multiple_of(qi_ref[e] * TQ, TQ)
        k0 = pl.multiple_of(kj_ref[e] * TK, TK)
        s = lax.dot_general(head_cols(k_ref, hd, HEAD_SLOT, pl.ds(k0, TK)),
                            head_cols(q_ref, hd, HEAD_SLOT, pl.ds(q0, TQ)), _NT,
                            preferred_element_type=_f32)
        s = s + mask_ref[kind_ref[e]]
        s_buf[hd, slot] = s
        return jnp.max(s, axis=0, keepdims=True)

    def stage_b(hd, e, slot, m, cmax):
        m_prev = jnp.where(first_ref[e] != 0, -jnp.inf, m)
        m_new = jnp.maximum(m_prev, cmax)
        p_buf[hd, slot] = jnp.exp2(s_buf[hd, slot] - m_new).astype(_bf16)
        return m_new, jnp.exp2(m_prev - m_new)

    ones_rows = jnp.ones((SUM_ROWS, TK), _bf16)

    def stage_c(hd, e, slot, acc, l, alpha):
        k0 = pl.multiple_of(kj_ref[e] * TK, TK)
        vt = vt_ref[hd * V_HEAD_DIM:(hd + 1) * V_HEAD_DIM, pl.ds(k0, TK)]
        pv = _dot(jnp.concatenate([vt, ones_rows], axis=0), p_buf[hd, slot])
        return alpha * acc + pv[:V_HEAD_DIM], alpha * l + pv[V_HEAD_DIM:V_HEAD_DIM + 1]

    def iteration(it, parity, carry):
        state = [dict(c) for c in carry]
        alpha_new = [list(st["alpha"]) for st in state]
        for u in range(U):
            for hd in range(ATT_HEADS):
                st = state[hd]
                st["m"], alpha_new[hd][u] = stage_b(hd, (it + 1) * U + u, (1 - parity) * U + u,
                                                    st["m"], st["cmax"][u])
        done = []
        for u in range(U):
            e = it * U + u
            for hd in range(ATT_HEADS):
                st = state[hd]
                st["acc"], st["l"] = stage_c(hd, e, parity * U + u, st["acc"], st["l"],
                                             st["alpha"][u])
            done.append((e, [(st["acc"], st["l"]) for st in state]))
        cmax_new = [[None] * U for _ in range(ATT_HEADS)]
        for u in range(U):
            for hd in range(ATT_HEADS):
                cmax_new[hd][u] = stage_a(hd, (it + 2) * U + u, parity * U + u)
        for hd in range(ATT_HEADS):
            state[hd]["alpha"] = tuple(alpha_new[hd])
            state[hd]["cmax"] = tuple(cmax_new[hd])
        for e, heads in done:
            @pl.when(last_ref[e] != 0)
            def _():
                q0 = pl.multiple_of(qi_ref[e] * TQ, TQ)
                for hd, (acc, l) in enumerate(heads):
                    o_ref[hd * V_HEAD_DIM:(hd + 1) * V_HEAD_DIM, pl.ds(q0, TQ)] = (
                        (acc / l).astype(_bf16))
        return tuple(state)

    def two_iterations(i, carry):
        return iteration(2 * i + 1, 1, iteration(2 * i, 0, carry))

    zeros = jnp.zeros((1, TQ), _f32)
    init = dict(m=zeros, l=zeros, acc=jnp.zeros((V_HEAD_DIM, TQ), _f32),
                alpha=(zeros,) * U, cmax=(zeros,) * U)
    lax.fori_loop(0, n_iter // 2, two_iterations, (init,) * ATT_HEADS)


def _post_kernel(attn_ref, sg_ref, gc_ref, x_ref, w_o_ref, w_out_ref, g_mix_post_ref,
                 g_mlp_pre_ref, w_ff1_ref, w_ff2_ref, g_mlp_post_ref, out_ref):
    tm = x_ref.shape[0]
    halves = [pl.ds(i * (tm // POST_SPLIT), tm // POST_SPLIT) for i in range(POST_SPLIT)]

    merged = []
    for rows in halves:
        y_attn = lax.dot_general(attn_ref[:, rows], w_o_ref[...], _TN,
                                 preferred_element_type=_f32)
        merged.append((sg_ref[rows, :].astype(_f32) * y_attn
                       + gc_ref[rows, :].astype(_f32)).astype(_bf16))
    x1, h2 = [], []
    for rows, mg in zip(halves, merged):
        x1.append(x_ref[rows, :] + _rms(_dot(mg, w_out_ref[...]), g_mix_post_ref[...]))
        h2.append(_rms(x1[-1], g_mlp_pre_ref[...]).astype(_bf16))
    f = [jnp.zeros(x.shape, _f32) for x in x1]
    for c in range(D_FF // FF_CHUNK):
        cols = slice(c * FF_CHUNK, (c + 1) * FF_CHUNK)
        for i in range(POST_SPLIT):
            hc = jnp.maximum(_dot(h2[i], w_ff1_ref[:, cols]), 0.0)
            f[i] = f[i] + _dot((hc * hc).astype(_bf16), w_ff2_ref[cols, :])
    for rows, xi, fi in zip(halves, x1, f):
        out_ref[rows, :] = xi + _rms(fi, g_mlp_post_ref[...])


def _resident(shape):
    return pl.BlockSpec(shape, lambda *_: (0,) * len(shape), pipeline_mode=pl.Buffered(1))


def _rows(tm, cols):
    return pl.BlockSpec((tm, cols), lambda i: (i, 0))


def kernel(x, positions, norm_mix_pre, w_in, q_norm, w_uq, kv_norm, w_uk, w_uv, w_o_attn,
           conv_w, conv_b, conv_ln_g, conv_ln_b, w_pw2, b_pw2, w_out, norm_mix_post,
           norm_mlp_pre, w_ff1, w_ff2, norm_mlp_post):
    B, S, D = x.shape
    T = B * S
    assert D == D_MODEL and w_in.shape[0] == 1
    assert S % TM_IN == 0 and S % TK == 0 and TK % TQ == 0 and S % TM_POST == 0
    assert N_HEADS % ATT_HEADS == 0
    C = CONV_CHANNELS
    H = N_HEADS
    tiles_per_seq = S // TM_IN

    w_in_t = jnp.swapaxes(w_in[0], 0, 1).astype(_bf16)
    o_kv, o_kr = Q_LORA_RANK + KV_LORA_RANK, Q_LORA_RANK + KV_LORA_RANK + QK_ROPE_DIM
    k1 = w_in_t[o_kv:o_kv + HALF_ROPE]
    k2 = w_in_t[o_kv + HALF_ROPE:o_kr]
    w_small = jnp.concatenate([w_in_t[:o_kv], k1, k2, k2, k1], axis=0)
    w_big = w_in_t[o_kr:]

    wq = w_uq[0].reshape(Q_LORA_RANK, H, QK_NOPE_DIM + QK_ROPE_DIM)
    r1 = wq[:, :, QK_NOPE_DIM:QK_NOPE_DIM + HALF_ROPE]
    r2 = wq[:, :, QK_NOPE_DIM + HALF_ROPE:]
    w_uq_k = jnp.concatenate([wq[:, :, :QK_NOPE_DIM], r1, r2, r2, r1], axis=2)
    w_uq_k = w_uq_k.reshape(Q_LORA_RANK, H * HEAD_SLOT).astype(_bf16)
    w_uk_k = w_uk[0].reshape(KV_LORA_RANK, H * QK_NOPE_DIM).astype(_bf16)
    w_uvt_k = w_uv[0].reshape(KV_LORA_RANK, H * V_HEAD_DIM).T.astype(_bf16)
    cw = jnp.pad(conv_w[0].reshape(CONV_WIDTH, C), ((0, 1), (0, 0)))
    cw = cw.reshape(CONV_WIDTH + 1, C // LANES, LANES).transpose(1, 0, 2)
    cb = conv_b[0].reshape(C // LANES, 1, LANES)

    inv_freq = ROPE_THETA ** (-jnp.arange(0, QK_ROPE_DIM, 2, dtype=_f32) / QK_ROPE_DIM)
    invf4 = jnp.tile(inv_freq, 4).reshape(1, ROPE_LANES)

    row = lambda a: a.reshape(1, -1)
    x2 = x.reshape(T, D)
    pos2 = positions.reshape(T // TM_IN, ROPE_PACK, TM_IN // ROPE_PACK).transpose(0, 2, 1)

    q, k, vt, sg, gc = pl.pallas_call(
        functools.partial(_inproj_kernel, tiles_per_seq=tiles_per_seq),
        grid=(T // TM_IN,),
        in_specs=[
            _rows(TM_IN, D),
            pl.BlockSpec((None, TM_IN // ROPE_PACK, ROPE_PACK), lambda i: (i, 0, 0)),
            _resident((1, ROPE_LANES)), _resident((1, D)),
            _resident(w_small.shape), _resident(w_big.shape),
            _resident((1, Q_LORA_RANK)), _resident(w_uq_k.shape),
            _resident((1, KV_LORA_RANK)), _resident(w_uk_k.shape), _resident(w_uvt_k.shape),
            _resident(cw.shape), _resident(cb.shape), _resident((1, C)), _resident((1, C)),
            _resident((C, D)), _resident((1, D)),
        ],
        out_specs=[_rows(TM_IN, H * HEAD_SLOT), _rows(TM_IN, H * HEAD_SLOT),
                   pl.BlockSpec((None, H * V_HEAD_DIM, TM_IN),
                                lambda i: (i // tiles_per_seq, 0, i % tiles_per_seq)),
                   _rows(TM_IN, D), _rows(TM_IN, D)],
        out_shape=[jax.ShapeDtypeStruct((T, H * HEAD_SLOT), _bf16),
                   jax.ShapeDtypeStruct((T, H * HEAD_SLOT), _bf16),
                   jax.ShapeDtypeStruct((B, H * V_HEAD_DIM, S), _bf16),
                   jax.ShapeDtypeStruct((T, D), _bf16),
                   jax.ShapeDtypeStruct((T, D), _bf16)],
        scratch_shapes=[pltpu.VMEM((C // LANES, HALO + TM_IN, LANES), _f32),
                        pltpu.VMEM((C // LANES, TM_IN, LANES), _f32),
                        pltpu.VMEM((TM_IN, D), _bf16)],
        compiler_params=pltpu.CompilerParams(dimension_semantics=("arbitrary",),
                                             vmem_limit_bytes=VMEM_LIMIT),
        name="inproj",
    )(x2, pos2, invf4, row(norm_mix_pre[0]), w_small, w_big, row(q_norm[0]), w_uq_k, row(kv_norm[0]),
      w_uk_k, w_uvt_k, cw, cb, row(conv_ln_g[0]), row(conv_ln_b[0]),
      w_pw2[0].astype(_bf16), row(b_pw2[0]))

    n_blocks, tables = _attn_stream(S)
    assert n_blocks % (2 * ATT_UNROLL) == 0
    n_iter = n_blocks // ATT_UNROLL + 2
    mask_tiles = _attn_mask_tiles()
    qk_spec = pl.BlockSpec((S, ATT_HEADS * HEAD_SLOT), lambda b, g, *_: (b, g))
    attn = pl.pallas_call(
        functools.partial(_attn_kernel, n_iter=n_iter),
        grid_spec=pltpu.PrefetchScalarGridSpec(
            num_scalar_prefetch=len(tables),
            grid=(B, H // ATT_HEADS),
            in_specs=[qk_spec, qk_spec,
                      pl.BlockSpec((None, ATT_HEADS * V_HEAD_DIM, S), lambda b, g, *_: (b, g, 0)),
                      pl.BlockSpec(mask_tiles.shape, lambda b, g, *_: (0, 0, 0))],
            out_specs=pl.BlockSpec((None, ATT_HEADS * V_HEAD_DIM, S), lambda b, g, *_: (b, g, 0)),
            scratch_shapes=[pltpu.VMEM((ATT_HEADS, 2 * ATT_UNROLL, TK, TQ), _f32),
                            pltpu.VMEM((ATT_HEADS, 2 * ATT_UNROLL, TK, TQ), _bf16)]),
        out_shape=jax.ShapeDtypeStruct((B, H * V_HEAD_DIM, S), _bf16),
        compiler_params=pltpu.CompilerParams(dimension_semantics=("arbitrary", "arbitrary"),
                                             vmem_limit_bytes=VMEM_LIMIT),
        name="attn",
    )(*[jnp.asarray(t) for t in tables], q, k, vt, jnp.asarray(mask_tiles))

    out = pl.pallas_call(
        _post_kernel,
        grid=(T // TM_POST,),
        in_specs=[pl.BlockSpec((None, H * V_HEAD_DIM, TM_POST),
                               lambda i: (i // (S // TM_POST), 0, i % (S // TM_POST))),
                  _rows(TM_POST, D), _rows(TM_POST, D), _rows(TM_POST, D), _resident((H * V_HEAD_DIM, D)), _resident((D, D)),
                  _resident((1, D)), _resident((1, D)), _resident((D, D_FF)),
                  _resident((D_FF, D)), _resident((1, D))],
        out_specs=_rows(TM_POST, D),
        out_shape=jax.ShapeDtypeStruct((T, D), _f32),
        compiler_params=pltpu.CompilerParams(dimension_semantics=("arbitrary",),
                                             vmem_limit_bytes=VMEM_LIMIT),
        name="post",
    )(attn, sg, gc, x2, w_o_attn[0].astype(_bf16), w_out[0].astype(_bf16),
      row(norm_mix_post[0]), row(norm_mlp_pre[0]), w_ff1[0].astype(_bf16),
      w_ff2[0].astype(_bf16), row(norm_mlp_post[0]))
    return out.reshape(B, S, D)
```

```python
import functools
import math

import jax
import jax.numpy as jnp
import numpy as np
from jax import lax
from jax.experimental import pallas as pl
from jax.experimental.pallas import tpu as pltpu

D_MODEL = 1024
N_HEADS = 8
QK_NOPE_DIM = 128
QK_ROPE_DIM = 64
V_HEAD_DIM = 128
Q_LORA_RANK = 384
KV_LORA_RANK = 256
ROPE_THETA = 10000.0
CONV_CHANNELS = 1024
CONV_WIDTH = 31
D_FF = 4096
EPS = 1e-6

HALF_ROPE = QK_ROPE_DIM // 2
ROPE_PACK = 4
ROPE_LANES = ROPE_PACK * HALF_ROPE
HEAD_SLOT = QK_NOPE_DIM + ROPE_LANES
SMALL_COLS = Q_LORA_RANK + KV_LORA_RANK + ROPE_LANES
BIG_ROW0 = Q_LORA_RANK + KV_LORA_RANK + QK_ROPE_DIM
HALO = 32
SUBLANES = 8
LANES = 128
CONV_ROWS = 64
CONV_STRIDE = 2

Q_SCALE = (QK_NOPE_DIM + QK_ROPE_DIM) ** -0.5 * math.log2(math.e)
MASK_VALUE = -1e30

TM_IN = 512
TM_POST = 512
TQ = 256
TK = 512
ATT_HEADS = 2
ATT_UNROLL = 2
SUM_ROWS = 16
FF_CHUNK = 1024
POST_SPLIT = 2

VMEM_LIMIT = 60 * 1024 * 1024

_bf16 = jnp.bfloat16
_f32 = jnp.float32
_NT = (((1,), (1,)), ((), ()))
_TN = (((0,), (0,)), ((), ()))


def _dot(a, b):
    return jnp.dot(a, b, preferred_element_type=_f32)


def _dot_nt(a, b):
    return lax.dot_general(a, b, _NT, preferred_element_type=_f32)


def _rms(x, g):
    return x * lax.rsqrt(jnp.mean(x * x, axis=-1, keepdims=True) + EPS) * g


def _inproj_kernel(x_ref, pos_ref, invf_ref, g_pre_ref, w_small_ref, w_big_ref, qn_g_ref, w_uq_ref,
                   kvn_g_ref, w_uk_ref, w_uvt_ref, cw_ref, cb_ref, lng_ref, lnb_ref,
                   w_pw2_ref, b_pw2_ref,
                   q_ref, k_ref, vt_ref, sg_ref, gc_ref,
                   ubuf, cacc, hbuf, *, tiles_per_seq):
    tm = x_ref.shape[0]
    C = CONV_CHANNELS

    n_slabs = C // LANES

    @pl.when(pl.program_id(0) % tiles_per_seq == 0)
    def _():
        ubuf[:, 0:HALO, :] = jnp.zeros((n_slabs, HALO, LANES), _f32)

    hbuf[...] = _rms(x_ref[...], g_pre_ref[...]).astype(_bf16)
    gate = jax.nn.sigmoid(_dot_nt(hbuf[...], w_big_ref[BIG_ROW0 + C:BIG_ROW0 + 2 * C, :]))
    u = _dot_nt(hbuf[...], w_big_ref[BIG_ROW0:BIG_ROW0 + C, :]) * gate
    for c in range(n_slabs):
        ubuf[c, HALO:HALO + tm, :] = u[:, c * LANES:(c + 1) * LANES]

    shift = HALO - (CONV_WIDTH - 1)
    group = CONV_STRIDE * SUBLANES

    def conv_slab(c, carry):
        for r0 in range(0, tm, CONV_ROWS):
            starts = [r0 + g * group + p for g in range(CONV_ROWS // group)
                      for p in range(CONV_STRIDE)]
            accs = [jnp.broadcast_to(cb_ref[c], (SUBLANES, LANES)) for _ in starts]
            for t in range(CONV_WIDTH):
                w = cw_ref[c, pl.ds(t, 1), :]
                for j, start in enumerate(starts):
                    rows = pl.ds(start + shift + t, SUBLANES, stride=CONV_STRIDE)
                    accs[j] = accs[j] + ubuf[c, rows, :] * w
            for acc, start in zip(accs, starts):
                cacc[c, pl.ds(start, SUBLANES, stride=CONV_STRIDE), :] = acc
        return carry

    lax.fori_loop(0, n_slabs, conv_slab, 0)
    ubuf[:, 0:HALO, :] = ubuf[:, tm:tm + HALO, :]

    conv = jnp.concatenate([cacc[c] for c in range(n_slabs)], axis=1)
    xc = conv - jnp.mean(conv, axis=-1, keepdims=True)
    var = jnp.mean(xc * xc, axis=-1, keepdims=True)
    y = xc * lax.rsqrt(var + EPS) * lng_ref[...] + lnb_ref[...]
    swish = (y * jax.nn.sigmoid(y)).astype(_bf16)

    h = hbuf[...]
    zs = _dot_nt(h, w_small_ref[...])
    cq = zs[:, 0:Q_LORA_RANK]
    ckv = zs[:, Q_LORA_RANK:Q_LORA_RANK + KV_LORA_RANK]
    kr4 = zs[:, Q_LORA_RANK + KV_LORA_RANK:SMALL_COLS]
    sg_ref[...] = jax.nn.sigmoid(
        _dot_nt(h, w_big_ref[BIG_ROW0 + 2 * C:BIG_ROW0 + 3 * C, :])).astype(_bf16)

    pos = pos_ref[...].astype(_f32)
    lane = lax.broadcasted_iota(jnp.int32, (tm // ROPE_PACK, ROPE_LANES), 1)
    in_group = [lane < (g + 1) * HALF_ROPE for g in range(ROPE_PACK - 1)]

    def by_group(parts):
        out = parts[-1]
        for g in reversed(range(ROPE_PACK - 1)):
            out = jnp.where(in_group[g], parts[g], out)
        return out

    ang = by_group([pos[:, g:g + 1] for g in range(ROPE_PACK)]) * invf_ref[...]
    cos, sin = jnp.cos(ang), jnp.sin(ang)
    cos_rot = [cos] + [pltpu.roll(cos, g * HALF_ROPE, 1) for g in range(1, ROPE_PACK)]
    sin_rot = [sin] + [pltpu.roll(sin, g * HALF_ROPE, 1) for g in range(1, ROPE_PACK)]
    rope4 = jnp.concatenate(
        [by_group([cos_rot[(0 - g) % ROPE_PACK], cos_rot[(1 - g) % ROPE_PACK],
                   -sin_rot[(2 - g) % ROPE_PACK], sin_rot[(3 - g) % ROPE_PACK]])
         for g in range(ROPE_PACK)], axis=0)

    q = _dot(_rms(cq, qn_g_ref[...]).astype(_bf16), w_uq_ref[...])
    rope4_q = rope4 * Q_SCALE
    for hd in range(N_HEADS):
        lo = hd * HEAD_SLOT
        q_ref[:, lo:lo + QK_NOPE_DIM] = (q[:, lo:lo + QK_NOPE_DIM] * Q_SCALE).astype(_bf16)
        q_ref[:, lo + QK_NOPE_DIM:lo + HEAD_SLOT] = (
            q[:, lo + QK_NOPE_DIM:lo + HEAD_SLOT] * rope4_q).astype(_bf16)

    g_conv = jax.nn.sigmoid(_dot_nt(h, w_big_ref[BIG_ROW0 + 3 * C:BIG_ROW0 + 4 * C, :]))

    kp = kr4 * rope4
    kdup = (kp + pltpu.roll(kp, 2 * HALF_ROPE, 1)).astype(_bf16)
    ckvn = _rms(ckv, kvn_g_ref[...]).astype(_bf16)
    kn = _dot(ckvn, w_uk_ref[...])
    for hd in range(N_HEADS):
        lo = hd * HEAD_SLOT
        k_ref[:, lo:lo + QK_NOPE_DIM] = kn[:, hd * QK_NOPE_DIM:(hd + 1) * QK_NOPE_DIM].astype(_bf16)
        k_ref[:, lo + QK_NOPE_DIM:lo + HEAD_SLOT] = kdup
    vt_ref[...] = _dot_nt(w_uvt_ref[...], ckvn).astype(_bf16)

    y_conv = _dot(swish, w_pw2_ref[...]) + b_pw2_ref[...]
    gc_ref[...] = (g_conv * y_conv).astype(_bf16)


def _attn_stream(seq):
    qi, kj, first, last, kind = [], [], [], [], []
    for i in range(seq // TQ):
        n_blocks = (i * TQ + TQ - 1) // TK + 1
        for j in range(n_blocks):
            qi.append(i)
            kj.append(j)
            first.append(int(j == 0))
            last.append(int(j == n_blocks - 1))
            kind.append(0 if j < n_blocks - 1 else 1 + (i * TQ - j * TK) // TQ)
    pad = [0] * (2 * ATT_UNROLL)
    tables = [np.asarray(pad + t + pad, np.int32) for t in (qi, kj, first, last, kind)]
    return len(qi), tables


def _attn_mask_tiles():
    kk = np.arange(TK)[:, None]
    qq = np.arange(TQ)[None, :]
    tiles = [np.zeros((TK, TQ), np.float32)]
    for d in range(TK // TQ):
        tiles.append(np.where(kk <= qq + d * TQ, 0.0, MASK_VALUE).astype(np.float32))
    return np.stack(tiles)


def _attn_kernel(qi_ref, kj_ref, first_ref, last_ref, kind_ref,
                 q_ref, k_ref, vt_ref, mask_ref, o_ref, s_buf, p_buf, *, n_iter):
    U = ATT_UNROLL
    s_buf[...] = jnp.zeros(s_buf.shape, _f32)
    p_buf[...] = jnp.zeros(p_buf.shape, _bf16)

    def head_cols(ref, hd, width, rows):
        return ref[rows, hd * width:(hd + 1) * width]

    def stage_a(hd, e, slot):
        q0 = pl.multiple_of(qi_ref[e] * TQ, TQ)
        k0 = pl.multiple_of(kj_ref[e] * TK, TK)
        s = lax.dot_general(head_cols(k_ref, hd, HEAD_SLOT, pl.ds(k0, TK)),
                            head_cols(q_ref, hd, HEAD_SLOT, pl.ds(q0, TQ)), _NT,
                            preferred_element_type=_f32)
        s = s + mask_ref[kind_ref[e]]
        s_buf[hd, slot] = s
        return jnp.max(s, axis=0, keepdims=True)

    def stage_b(hd, e, slot, m, cmax):
        m_prev = jnp.where(first_ref[e] != 0, -jnp.inf, m)
        m_new = jnp.maximum(m_prev, cmax)
        p_buf[hd, slot] = jnp.exp2(s_buf[hd, slot] - m_new).astype(_bf16)
        return m_new, jnp.exp2(m_prev - m_new)

    ones_rows = jnp.ones((SUM_ROWS, TK), _bf16)

    def stage_c(hd, e, slot, acc, l, alpha):
        k0 = pl.multiple_of(kj_ref[e] * TK, TK)
        vt = vt_ref[hd * V_HEAD_DIM:(hd + 1) * V_HEAD_DIM, pl.ds(k0, TK)]
        pv = _dot(jnp.concatenate([vt, ones_rows], axis=0), p_buf[hd, slot])
        return alpha * acc + pv[:V_HEAD_DIM], alpha * l + pv[V_HEAD_DIM:V_HEAD_DIM + 1]

    def iteration(it, parity, carry):
        state = [dict(c) for c in carry]
        alpha_new = [list(st["alpha"]) for st in state]
        for u in range(U):
            for hd in range(ATT_HEADS):
                st = state[hd]
                st["m"], alpha_new[hd][u] = stage_b(hd, (it + 1) * U + u, (1 - parity) * U + u,
                                                    st["m"], st["cmax"][u])
        cmax_new = [[None] * U for _ in range(ATT_HEADS)]
        for u in range(U):
            for hd in range(ATT_HEADS):
                cmax_new[hd][u] = stage_a(hd, (it + 2) * U + u, parity * U + u)
        done = []
        for u in range(U):
            e = it * U + u
            for hd in range(ATT_HEADS):
                st = state[hd]
                st["acc"], st["l"] = stage_c(hd, e, parity * U + u, st["acc"], st["l"],
                                             st["alpha"][u])
            done.append((e, [(st["acc"], st["l"]) for st in state]))
        for hd in range(ATT_HEADS):
            state[hd]["alpha"] = tuple(alpha_new[hd])
            state[hd]["cmax"] = tuple(cmax_new[hd])
        for e, heads in done:
            @pl.when(last_ref[e] != 0)
            def _():
                q0 = pl.multiple_of(qi_ref[e] * TQ, TQ)
                for hd, (acc, l) in enumerate(heads):
                    o_ref[hd * V_HEAD_DIM:(hd + 1) * V_HEAD_DIM, pl.ds(q0, TQ)] = (
                        (acc / l).astype(_bf16))
        return tuple(state)

    def two_iterations(i, carry):
        return iteration(2 * i + 1, 1, iteration(2 * i, 0, carry))

    zeros = jnp.zeros((1, TQ), _f32)
    init = dict(m=zeros, l=zeros, acc=jnp.zeros((V_HEAD_DIM, TQ), _f32),
                alpha=(zeros,) * U, cmax=(zeros,) * U)
    lax.fori_loop(0, n_iter // 2, two_iterations, (init,) * ATT_HEADS)


def _post_kernel(attn_ref, sg_ref, gc_ref, x_ref, w_o_ref, w_out_ref, g_mix_post_ref,
                 g_mlp_pre_ref, w_ff1_ref, w_ff2_ref, g_mlp_post_ref, out_ref):
    tm = x_ref.shape[0]
    halves = [pl.ds(i * (tm // POST_SPLIT), tm // POST_SPLIT) for i in range(POST_SPLIT)]

    merged = []
    for rows in halves:
        y_attn = lax.dot_general(attn_ref[:, rows], w_o_ref[...], _TN,
                                 preferred_element_type=_f32)
        merged.append((sg_ref[rows, :].astype(_f32) * y_attn
                       + gc_ref[rows, :].astype(_f32)).astype(_bf16))
    x1, h2 = [], []
    for rows, mg in zip(halves, merged):
        x1.append(x_ref[rows, :] + _rms(_dot(mg, w_out_ref[...]), g_mix_post_ref[...]))
        h2.append(_rms(x1[-1], g_mlp_pre_ref[...]).astype(_bf16))
    f = [jnp.zeros(x.shape, _f32) for x in x1]
    for c in range(D_FF // FF_CHUNK):
        cols = slice(c * FF_CHUNK, (c + 1) * FF_CHUNK)
        for i in range(POST_SPLIT):
            hc = jnp.maximum(_dot(h2[i], w_ff1_ref[:, cols]), 0.0)
            f[i] = f[i] + _dot((hc * hc).astype(_bf16), w_ff2_ref[cols, :])
    for rows, xi, fi in zip(halves, x1, f):
        out_ref[rows, :] = xi + _rms(fi, g_mlp_post_ref[...])


def _resident(shape):
    return pl.BlockSpec(shape, lambda *_: (0,) * len(shape), pipeline_mode=pl.Buffered(1))


def _rows(tm, cols):
    return pl.BlockSpec((tm, cols), lambda i: (i, 0))


def kernel(x, positions, norm_mix_pre, w_in, q_norm, w_uq, kv_norm, w_uk, w_uv, w_o_attn,
           conv_w, conv_b, conv_ln_g, conv_ln_b, w_pw2, b_pw2, w_out, norm_mix_post,
           norm_mlp_pre, w_ff1, w_ff2, norm_mlp_post):
    B, S, D = x.shape
    T = B * S
    assert D == D_MODEL and w_in.shape[0] == 1
    assert S % TM_IN == 0 and S % TK == 0 and TK % TQ == 0 and S % TM_POST == 0
    assert N_HEADS % ATT_HEADS == 0
    C = CONV_CHANNELS
    H = N_HEADS
    tiles_per_seq = S // TM_IN

    w_in_t = jnp.swapaxes(w_in[0], 0, 1).astype(_bf16)
    o_kv, o_kr = Q_LORA_RANK + KV_LORA_RANK, Q_LORA_RANK + KV_LORA_RANK + QK_ROPE_DIM
    k1 = w_in_t[o_kv:o_kv + HALF_ROPE]
    k2 = w_in_t[o_kv + HALF_ROPE:o_kr]
    w_small = jnp.concatenate([w_in_t[:o_kv], k1, k2, k2, k1], axis=0)
    w_big = w_in_t

    wq = w_uq[0].reshape(Q_LORA_RANK, H, QK_NOPE_DIM + QK_ROPE_DIM)
    r1 = wq[:, :, QK_NOPE_DIM:QK_NOPE_DIM + HALF_ROPE]
    r2 = wq[:, :, QK_NOPE_DIM + HALF_ROPE:]
    w_uq_k = jnp.concatenate([wq[:, :, :QK_NOPE_DIM], r1, r2, r2, r1], axis=2)
    w_uq_k = w_uq_k.reshape(Q_LORA_RANK, H * HEAD_SLOT).astype(_bf16)
    w_uk_k = w_uk[0].reshape(KV_LORA_RANK, H * QK_NOPE_DIM).astype(_bf16)
    w_uvt_k = w_uv[0].reshape(KV_LORA_RANK, H * V_HEAD_DIM).T.astype(_bf16)
    cw = jnp.pad(conv_w[0].reshape(CONV_WIDTH, C), ((0, 1), (0, 0)))
    cw = cw.reshape(CONV_WIDTH + 1, C // LANES, LANES).transpose(1, 0, 2)
    cb = conv_b[0].reshape(C // LANES, 1, LANES)

    inv_freq = ROPE_THETA ** (-jnp.arange(0, QK_ROPE_DIM, 2, dtype=_f32) / QK_ROPE_DIM)
    invf4 = jnp.tile(inv_freq, 4).reshape(1, ROPE_LANES)

    row = lambda a: a.reshape(1, -1)
    x2 = x.reshape(T, D)
    pos2 = positions.reshape(T // TM_IN, ROPE_PACK, TM_IN // ROPE_PACK).transpose(0, 2, 1)

    q, k, vt, sg, gc = pl.pallas_call(
        functools.partial(_inproj_kernel, tiles_per_seq=tiles_per_seq),
        grid=(T // TM_IN,),
        in_specs=[
            _rows(TM_IN, D),
            pl.BlockSpec((None, TM_IN // ROPE_PACK, ROPE_PACK), lambda i: (i, 0, 0)),
            _resident((1, ROPE_LANES)), _resident((1, D)),
            _resident(w_small.shape), _resident(w_big.shape),
            _resident((1, Q_LORA_RANK)), _resident(w_uq_k.shape),
            _resident((1, KV_LORA_RANK)), _resident(w_uk_k.shape), _resident(w_uvt_k.shape),
            _resident(cw.shape), _resident(cb.shape), _resident((1, C)), _resident((1, C)),
            _resident((C, D)), _resident((1, D)),
        ],
        out_specs=[_rows(TM_IN, H * HEAD_SLOT), _rows(TM_IN, H * HEAD_SLOT),
                   pl.BlockSpec((None, H * V_HEAD_DIM, TM_IN),
                                lambda i: (i // tiles_per_seq, 0, i % tiles_per_seq)),
                   _rows(TM_IN, D), _rows(TM_IN, D)],
        out_shape=[jax.ShapeDtypeStruct((T, H * HEAD_SLOT), _bf16),
                   jax.ShapeDtypeStruct((T, H * HEAD_SLOT), _bf16),
                   jax.ShapeDtypeStruct((B, H * V_HEAD_DIM, S), _bf16),
                   jax.ShapeDtypeStruct((T, D), _bf16),
                   jax.ShapeDtypeStruct((T, D), _bf16)],
        scratch_shapes=[pltpu.VMEM((C // LANES, HALO + TM_IN, LANES), _f32),
                        pltpu.VMEM((C // LANES, TM_IN, LANES), _f32),
                        pltpu.VMEM((TM_IN, D), _bf16)],
        compiler_params=pltpu.CompilerParams(dimension_semantics=("arbitrary",),
                                             vmem_limit_bytes=VMEM_LIMIT),
        name="inproj",
    )(x2, pos2, invf4, row(norm_mix_pre[0]), w_small, w_big, row(q_norm[0]), w_uq_k, row(kv_norm[0]),
      w_uk_k, w_uvt_k, cw, cb, row(conv_ln_g[0]), row(conv_ln_b[0]),
      w_pw2[0].astype(_bf16), row(b_pw2[0]))

    n_blocks, tables = _attn_stream(S)
    assert n_blocks % (2 * ATT_UNROLL) == 0
    n_iter = n_blocks // ATT_UNROLL + 2
    mask_tiles = _attn_mask_tiles()
    qk_spec = pl.BlockSpec((S, ATT_HEADS * HEAD_SLOT), lambda b, g, *_: (b, g))
    attn = pl.pallas_call(
        functools.partial(_attn_kernel, n_iter=n_iter),
        grid_spec=pltpu.PrefetchScalarGridSpec(
            num_scalar_prefetch=len(tables),
            grid=(B, H // ATT_HEADS),
            in_specs=[qk_spec, qk_spec,
                      pl.BlockSpec((None, ATT_HEADS * V_HEAD_DIM, S), lambda b, g, *_: (b, g, 0)),
                      pl.BlockSpec(mask_tiles.shape, lambda b, g, *_: (0, 0, 0))],
            out_specs=pl.BlockSpec((None, ATT_HEADS * V_HEAD_DIM, S), lambda b, g, *_: (b, g, 0)),
            scratch_shapes=[pltpu.VMEM((ATT_HEADS, 2 * ATT_UNROLL, TK, TQ), _f32),
                            pltpu.VMEM((ATT_HEADS, 2 * ATT_UNROLL, TK, TQ), _bf16)]),
        out_shape=jax.ShapeDtypeStruct((B, H * V_HEAD_DIM, S), _bf16),
        compiler_params=pltpu.CompilerParams(dimension_semantics=("arbitrary", "arbitrary"),
                                             vmem_limit_bytes=VMEM_LIMIT),
        name="attn",
    )(*[jnp.asarray(t) for t in tables], q, k, vt, jnp.asarray(mask_tiles))

    out = pl.pallas_call(
        _post_kernel,
        grid=(T // TM_POST,),
        in_specs=[pl.BlockSpec((None, H * V_HEAD_DIM, TM_POST),
                               lambda i: (i // (S // TM_POST), 0, i % (S // TM_POST))),
                  _rows(TM_POST, D), _rows(TM_POST, D), _rows(TM_POST, D), _resident((H * V_HEAD_DIM, D)), _resident((D, D)),
                  _resident((1, D)), _resident((1, D)), _resident((D, D_FF)),
                  _resident((D_FF, D)), _resident((1, D))],
        out_specs=_rows(TM_POST, D),
        out_shape=jax.ShapeDtypeStruct((T, D), _f32),
        compiler_params=pltpu.CompilerParams(dimension_semantics=("arbitrary",),
                                             vmem_limit_bytes=VMEM_LIMIT),
        name="post",
    )(attn, sg, gc, x2, w_o_attn[0].astype(_bf16), w_out[0].astype(_bf16),
      row(norm_mix_post[0]), row(norm_mlp_pre[0]), w_ff1[0].astype(_bf16),
      w_ff2[0].astype(_bf16), row(norm_mlp_post[0]))
    return out.reshape(B, S, D)
```

```python
import functools
import math

import jax
import jax.numpy as jnp
import numpy as np
from jax import lax
from jax.experimental import pallas as pl
from jax.experimental.pallas import tpu as pltpu

D_MODEL = 1024
N_HEADS = 8
QK_NOPE_DIM = 128
QK_ROPE_DIM = 64
V_HEAD_DIM = 128
Q_LORA_RANK = 384
KV_LORA_RANK = 256
ROPE_THETA = 10000.0
CONV_CHANNELS = 1024
CONV_WIDTH = 31
D_FF = 4096
EPS = 1e-6

HALF_ROPE = QK_ROPE_DIM // 2
ROPE_PACK = 4
ROPE_LANES = ROPE_PACK * HALF_ROPE
HEAD_SLOT = QK_NOPE_DIM + ROPE_LANES
SMALL_COLS = Q_LORA_RANK + KV_LORA_RANK + ROPE_LANES
BIG_ROW0 = Q_LORA_RANK + KV_LORA_RANK + QK_ROPE_DIM
HALO = 32
SUBLANES = 8
LANES = 128
CONV_ROWS = 64
CONV_STRIDE = 2

Q_SCALE = (QK_NOPE_DIM + QK_ROPE_DIM) ** -0.5 * math.log2(math.e)
MASK_VALUE = -1e30

TM_IN = 512
TM_POST = 512
TQ = 256
TK = 512
ATT_HEADS = 2
ATT_UNROLL = 2
SUM_ROWS = 16
FF_CHUNK = 1024
POST_SPLIT = 2

V7X_VMEM_BYTES = 64 * 1024 * 1024
MIB = 1024 * 1024
VMEM_LIMIT = 46 * MIB
VMEM_LIMIT_ATTN = 36 * MIB
assert VMEM_LIMIT < V7X_VMEM_BYTES

_bf16 = jnp.bfloat16
_f32 = jnp.float32
_NT = (((1,), (1,)), ((), ()))
_TN = (((0,), (0,)), ((), ()))


def _dot(a, b):
    return jnp.dot(a, b, preferred_element_type=_f32)


def _dot_nt(a, b):
    return lax.dot_general(a, b, _NT, preferred_element_type=_f32)


def _rms(x, g):
    return x * lax.rsqrt(jnp.mean(x * x, axis=-1, keepdims=True) + EPS) * g


def _inproj_kernel(x_ref, pos_ref, invf_ref, g_pre_ref, w_small_ref, w_big_ref, qn_g_ref, w_uq_ref,
                   kvn_g_ref, w_uk_ref, w_uvt_ref, cw_ref, cb_ref, lng_ref, lnb_ref,
                   w_pw2_ref, b_pw2_ref,
                   q_ref, k_ref, vt_ref, sg_ref, gc_ref,
                   ubuf, cacc, hbuf, *, tiles_per_seq):
    tm = x_ref.shape[0]
    C = CONV_CHANNELS

    n_slabs = C // LANES

    @pl.when(pl.program_id(0) % tiles_per_seq == 0)
    def _():
        ubuf[:, 0:HALO, :] = jnp.zeros((n_slabs, HALO, LANES), _f32)

    hbuf[...] = _rms(x_ref[...], g_pre_ref[...]).astype(_bf16)
    gate = jax.nn.sigmoid(_dot_nt(hbuf[...], w_big_ref[BIG_ROW0 + C:BIG_ROW0 + 2 * C, :]))
    u = _dot_nt(hbuf[...], w_big_ref[BIG_ROW0:BIG_ROW0 + C, :]) * gate
    for c in range(n_slabs):
        ubuf[c, HALO:HALO + tm, :] = u[:, c * LANES:(c + 1) * LANES]

    shift = HALO - (CONV_WIDTH - 1)
    group = CONV_STRIDE * SUBLANES

    def conv_slab(c, carry):
        for r0 in range(0, tm, CONV_ROWS):
            starts = [r0 + g * group + p for g in range(CONV_ROWS // group)
                      for p in range(CONV_STRIDE)]
            accs = [jnp.broadcast_to(cb_ref[c], (SUBLANES, LANES)) for _ in starts]
            for t in range(CONV_WIDTH):
                w = cw_ref[c, pl.ds(t, 1), :]
                for j, start in enumerate(starts):
                    rows = pl.ds(start + shift + t, SUBLANES, stride=CONV_STRIDE)
                    accs[j] = accs[j] + ubuf[c, rows, :] * w
            for acc, start in zip(accs, starts):
                cacc[c, pl.ds(start, SUBLANES, stride=CONV_STRIDE), :] = acc
        return carry

    lax.fori_loop(0, n_slabs, conv_slab, 0)
    ubuf[:, 0:HALO, :] = ubuf[:, tm:tm + HALO, :]

    conv = jnp.concatenate([cacc[c] for c in range(n_slabs)], axis=1)
    xc = conv - jnp.mean(conv, axis=-1, keepdims=True)
    var = jnp.mean(xc * xc, axis=-1, keepdims=True)
    y = xc * lax.rsqrt(var + EPS) * lng_ref[...] + lnb_ref[...]
    swish = (y * jax.nn.sigmoid(y)).astype(_bf16)

    h = hbuf[...]
    zs = _dot_nt(h, w_small_ref[...])
    cq = zs[:, 0:Q_LORA_RANK]
    ckv = zs[:, Q_LORA_RANK:Q_LORA_RANK + KV_LORA_RANK]
    kr4 = zs[:, Q_LORA_RANK + KV_LORA_RANK:SMALL_COLS]
    sg_ref[...] = jax.nn.sigmoid(
        _dot_nt(h, w_big_ref[BIG_ROW0 + 2 * C:BIG_ROW0 + 3 * C, :])).astype(_bf16)

    pos = pos_ref[...].astype(_f32)
    lane = lax.broadcasted_iota(jnp.int32, (tm // ROPE_PACK, ROPE_LANES), 1)
    in_group = [lane < (g + 1) * HALF_ROPE for g in range(ROPE_PACK - 1)]

    def by_group(parts):
        out = parts[-1]
        for g in reversed(range(ROPE_PACK - 1)):
            out = jnp.where(in_group[g], parts[g], out)
        return out

    ang = by_group([pos[:, g:g + 1] for g in range(ROPE_PACK)]) * invf_ref[...]
    cos, sin = jnp.cos(ang), jnp.sin(ang)
    cos_rot = [cos] + [pltpu.roll(cos, g * HALF_ROPE, 1) for g in range(1, ROPE_PACK)]
    sin_rot = [sin] + [pltpu.roll(sin, g * HALF_ROPE, 1) for g in range(1, ROPE_PACK)]
    rope4 = jnp.concatenate(
        [by_group([cos_rot[(0 - g) % ROPE_PACK], cos_rot[(1 - g) % ROPE_PACK],
                   -sin_rot[(2 - g) % ROPE_PACK], sin_rot[(3 - g) % ROPE_PACK]])
         for g in range(ROPE_PACK)], axis=0)

    q = _dot(_rms(cq, qn_g_ref[...]).astype(_bf16), w_uq_ref[...])
    rope4_q = rope4 * Q_SCALE
    for hd in range(N_HEADS):
        lo = hd * HEAD_SLOT
        q_ref[:, lo:lo + QK_NOPE_DIM] = (q[:, lo:lo + QK_NOPE_DIM] * Q_SCALE).astype(_bf16)
        q_ref[:, lo + QK_NOPE_DIM:lo + HEAD_SLOT] = (
            q[:, lo + QK_NOPE_DIM:lo + HEAD_SLOT] * rope4_q).astype(_bf16)

    g_conv = jax.nn.sigmoid(_dot_nt(h, w_big_ref[BIG_ROW0 + 3 * C:BIG_ROW0 + 4 * C, :]))

    kp = kr4 * rope4
    kdup = (kp + pltpu.roll(kp, 2 * HALF_ROPE, 1)).astype(_bf16)
    ckvn = _rms(ckv, kvn_g_ref[...]).astype(_bf16)
    kn = _dot(ckvn, w_uk_ref[...])
    for hd in range(N_HEADS):
        lo = hd * HEAD_SLOT
        k_ref[:, lo:lo + QK_NOPE_DIM] = kn[:, hd * QK_NOPE_DIM:(hd + 1) * QK_NOPE_DIM].astype(_bf16)
        k_ref[:, lo + QK_NOPE_DIM:lo + HEAD_SLOT] = kdup
    vt_ref[...] = _dot_nt(w_uvt_ref[...], ckvn).astype(_bf16)

    y_conv = _dot(swish, w_pw2_ref[...]) + b_pw2_ref[...]
    gc_ref[...] = (g_conv * y_conv).astype(_bf16)


def _attn_stream(seq):
    qi, kj, first, last, kind = [], [], [], [], []
    for i in range(seq // TQ):
        n_blocks = (i * TQ + TQ - 1) // TK + 1
        for j in range(n_blocks):
            qi.append(i)
            kj.append(j)
            first.append(int(j == 0))
            last.append(int(j == n_blocks - 1))
            kind.append(0 if j < n_blocks - 1 else 1 + (i * TQ - j * TK) // TQ)
    pad = [0] * (2 * ATT_UNROLL)
    tables = [np.asarray(pad + t + pad, np.int32) for t in (qi, kj, first, last, kind)]
    return len(qi), tables


def _attn_mask_tiles():
    kk = np.arange(TK)[:, None]
    qq = np.arange(TQ)[None, :]
    tiles = [np.zeros((TK, TQ), np.float32)]
    for d in range(TK // TQ):
        tiles.append(np.where(kk <= qq + d * TQ, 0.0, MASK_VALUE).astype(np.float32))
    return np.stack(tiles)


def _attn_kernel(qi_ref, kj_ref, first_ref, last_ref, kind_ref,
                 q_ref, k_ref, vt_ref, mask_ref, o_ref, s_buf, p_buf, *, n_iter):
    U = ATT_UNROLL
    s_buf[...] = jnp.zeros(s_buf.shape, _f32)
    p_buf[...] = jnp.zeros(p_buf.shape, _bf16)

    def head_cols(ref, hd, width, rows):
        return ref[rows, hd * width:(hd + 1) * width]

    def stage_a(hd, e, slot):
        q0 = pl.multiple_of(qi_ref[e] * TQ, TQ)
        k0 = pl.multiple_of(kj_ref[e] * TK, TK)
        s = lax.dot_general(head_cols(k_ref, hd, HEAD_SLOT, pl.ds(k0, TK)),
                            head_cols(q_ref, hd, HEAD_SLOT, pl.ds(q0, TQ)), _NT,
                            preferred_element_type=_f32)
        s = s + mask_ref[kind_ref[e]]
        s_buf[hd, slot] = s
        return jnp.max(s, axis=0, keepdims=True)

    def stage_b(hd, e, slot, m, cmax):
        m_prev = jnp.where(first_ref[e] != 0, -jnp.inf, m)
        m_new = jnp.maximum(m_prev, cmax)
        p_buf[hd, slot] = jnp.exp2(s_buf[hd, slot] - m_new).astype(_bf16)
        return m_new, jnp.exp2(m_prev - m_new)

    ones_rows = jnp.ones((SUM_ROWS, TK), _bf16)

    def stage_c(hd, e, slot, acc, l, alpha):
        k0 = pl.multiple_of(kj_ref[e] * TK, TK)
        vt = vt_ref[hd * V_HEAD_DIM:(hd + 1) * V_HEAD_DIM, pl.ds(k0, TK)]
        pv = _dot(jnp.concatenate([vt, ones_rows], axis=0), p_buf[hd, slot])
        return alpha * acc + pv[:V_HEAD_DIM], alpha * l + pv[V_HEAD_DIM:V_HEAD_DIM + 1]

    def iteration(it, parity, carry):
        state = [dict(c) for c in carry]
        alpha_new = [list(st["alpha"]) for st in state]
        for u in range(U):
            for hd in range(ATT_HEADS):
                st = state[hd]
                st["m"], alpha_new[hd][u] = stage_b(hd, (it + 1) * U + u, (1 - parity) * U + u,
                                                    st["m"], st["cmax"][u])
        cmax_new = [[None] * U for _ in range(ATT_HEADS)]
        for u in range(U):
            for hd in range(ATT_HEADS):
                cmax_new[hd][u] = stage_a(hd, (it + 2) * U + u, parity * U + u)
        done = []
        for u in range(U):
            e = it * U + u
            for hd in range(ATT_HEADS):
                st = state[hd]
                st["acc"], st["l"] = stage_c(hd, e, parity * U + u, st["acc"], st["l"],
                                             st["alpha"][u])
            done.append((e, [(st["acc"], st["l"]) for st in state]))
        for hd in range(ATT_HEADS):
            state[hd]["alpha"] = tuple(alpha_new[hd])
            state[hd]["cmax"] = tuple(cmax_new[hd])
        for e, heads in done:
            @pl.when(last_ref[e] != 0)
            def _():
                q0 = pl.multiple_of(qi_ref[e] * TQ, TQ)
                for hd, (acc, l) in enumerate(heads):
                    o_ref[hd * V_HEAD_DIM:(hd + 1) * V_HEAD_DIM, pl.ds(q0, TQ)] = (
                        (acc / l).astype(_bf16))
        return tuple(state)

    def two_iterations(i, carry):
        return iteration(2 * i + 1, 1, iteration(2 * i, 0, carry))

    zeros = jnp.zeros((1, TQ), _f32)
    init = dict(m=zeros, l=zeros, acc=jnp.zeros((V_HEAD_DIM, TQ), _f32),
                alpha=(zeros,) * U, cmax=(zeros,) * U)
    lax.fori_loop(0, n_iter // 2, two_iterations, (init,) * ATT_HEADS)


def _post_kernel(attn_ref, sg_ref, gc_ref, x_ref, w_o_ref, w_out_ref, g_mix_post_ref,
                 g_mlp_pre_ref, w_ff1_ref, w_ff2_ref, g_mlp_post_ref, out_ref):
    tm = x_ref.shape[0]
    halves = [pl.ds(i * (tm // POST_SPLIT), tm // POST_SPLIT) for i in range(POST_SPLIT)]

    merged = []
    for rows in halves:
        y_attn = lax.dot_general(attn_ref[:, rows], w_o_ref[...], _TN,
                                 preferred_element_type=_f32)
        merged.append((sg_ref[rows, :].astype(_f32) * y_attn
                       + gc_ref[rows, :].astype(_f32)).astype(_bf16))
    x1, h2 = [], []
    for rows, mg in zip(halves, merged):
        x1.append(x_ref[rows, :] + _rms(_dot(mg, w_out_ref[...]), g_mix_post_ref[...]))
        h2.append(_rms(x1[-1], g_mlp_pre_ref[...]).astype(_bf16))
    f = [jnp.zeros(x.shape, _f32) for x in x1]
    for c in range(D_FF // FF_CHUNK):
        cols = slice(c * FF_CHUNK, (c + 1) * FF_CHUNK)
        for i in range(POST_SPLIT):
            hc = jnp.maximum(_dot(h2[i], w_ff1_ref[:, cols]), 0.0)
            f[i] = f[i] + _dot((hc * hc).astype(_bf16), w_ff2_ref[cols, :])
    for rows, xi, fi in zip(halves, x1, f):
        out_ref[rows, :] = xi + _rms(fi, g_mlp_post_ref[...])


def _resident(shape):
    return pl.BlockSpec(shape, lambda *_: (0,) * len(shape), pipeline_mode=pl.Buffered(1))


def _rows(tm, cols):
    return pl.BlockSpec((tm, cols), lambda i: (i, 0))


def kernel(x, positions, norm_mix_pre, w_in, q_norm, w_uq, kv_norm, w_uk, w_uv, w_o_attn,
           conv_w, conv_b, conv_ln_g, conv_ln_b, w_pw2, b_pw2, w_out, norm_mix_post,
           norm_mlp_pre, w_ff1, w_ff2, norm_mlp_post):
    B, S, D = x.shape
    T = B * S
    assert D == D_MODEL and w_in.shape[0] == 1
    assert S % TM_IN == 0 and S % TK == 0 and TK % TQ == 0 and S % TM_POST == 0
    assert N_HEADS % ATT_HEADS == 0
    C = CONV_CHANNELS
    H = N_HEADS
    tiles_per_seq = S // TM_IN

    w_in_t = jnp.swapaxes(w_in[0], 0, 1).astype(_bf16)
    o_kv, o_kr = Q_LORA_RANK + KV_LORA_RANK, Q_LORA_RANK + KV_LORA_RANK + QK_ROPE_DIM
    k1 = w_in_t[o_kv:o_kv + HALF_ROPE]
    k2 = w_in_t[o_kv + HALF_ROPE:o_kr]
    w_small = jnp.concatenate([w_in_t[:o_kv], k1, k2, k2, k1], axis=0)
    w_big = w_in_t

    wq = w_uq[0].reshape(Q_LORA_RANK, H, QK_NOPE_DIM + QK_ROPE_DIM)
    r1 = wq[:, :, QK_NOPE_DIM:QK_NOPE_DIM + HALF_ROPE]
    r2 = wq[:, :, QK_NOPE_DIM + HALF_ROPE:]
    w_uq_k = jnp.concatenate([wq[:, :, :QK_NOPE_DIM], r1, r2, r2, r1], axis=2)
    w_uq_k = w_uq_k.reshape(Q_LORA_RANK, H * HEAD_SLOT).astype(_bf16)
    w_uk_k = w_uk[0].reshape(KV_LORA_RANK, H * QK_NOPE_DIM).astype(_bf16)
    w_uvt_k = w_uv[0].reshape(KV_LORA_RANK, H * V_HEAD_DIM).T.astype(_bf16)
    cw = jnp.pad(conv_w[0].reshape(CONV_WIDTH, C), ((0, 1), (0, 0)))
    cw = cw.reshape(CONV_WIDTH + 1, C // LANES, LANES).transpose(1, 0, 2)
    cb = conv_b[0].reshape(C // LANES, 1, LANES)

    inv_freq = ROPE_THETA ** (-jnp.arange(0, QK_ROPE_DIM, 2, dtype=_f32) / QK_ROPE_DIM)
    invf4 = jnp.tile(inv_freq, 4).reshape(1, ROPE_LANES)

    row = lambda a: a.reshape(1, -1)
    x2 = x.reshape(T, D)
    pos2 = positions.reshape(T // TM_IN, ROPE_PACK, TM_IN // ROPE_PACK).transpose(0, 2, 1)

    q, k, vt, sg, gc = pl.pallas_call(
        functools.partial(_inproj_kernel, tiles_per_seq=tiles_per_seq),
        grid=(T // TM_IN,),
        in_specs=[
            _rows(TM_IN, D),
            pl.BlockSpec((None, TM_IN // ROPE_PACK, ROPE_PACK), lambda i: (i, 0, 0)),
            _resident((1, ROPE_LANES)), _resident((1, D)),
            _resident(w_small.shape), _resident(w_big.shape),
            _resident((1, Q_LORA_RANK)), _resident(w_uq_k.shape),
            _resident((1, KV_LORA_RANK)), _resident(w_uk_k.shape), _resident(w_uvt_k.shape),
            _resident(cw.shape), _resident(cb.shape), _resident((1, C)), _resident((1, C)),
            _resident((C, D)), _resident((1, D)),
        ],
        out_specs=[_rows(TM_IN, H * HEAD_SLOT), _rows(TM_IN, H * HEAD_SLOT),
                   pl.BlockSpec((None, H * V_HEAD_DIM, TM_IN),
                                lambda i: (i // tiles_per_seq, 0, i % tiles_per_seq)),
                   _rows(TM_IN, D), _rows(TM_IN, D)],
        out_shape=[jax.ShapeDtypeStruct((T, H * HEAD_SLOT), _bf16),
                   jax.ShapeDtypeStruct((T, H * HEAD_SLOT), _bf16),
                   jax.ShapeDtypeStruct((B, H * V_HEAD_DIM, S), _bf16),
                   jax.ShapeDtypeStruct((T, D), _bf16),
                   jax.ShapeDtypeStruct((T, D), _bf16)],
        scratch_shapes=[pltpu.VMEM((C // LANES, HALO + TM_IN, LANES), _f32),
                        pltpu.VMEM((C // LANES, TM_IN, LANES), _f32),
                        pltpu.VMEM((TM_IN, D), _bf16)],
        compiler_params=pltpu.CompilerParams(dimension_semantics=("arbitrary",),
                                             vmem_limit_bytes=VMEM_LIMIT),
        name="inproj",
    )(x2, pos2, invf4, row(norm_mix_pre[0]), w_small, w_big, row(q_norm[0]), w_uq_k, row(kv_norm[0]),
      w_uk_k, w_uvt_k, cw, cb, row(conv_ln_g[0]), row(conv_ln_b[0]),
      w_pw2[0].astype(_bf16), row(b_pw2[0]))

    n_blocks, tables = _attn_stream(S)
    assert n_blocks % (2 * ATT_UNROLL) == 0
    n_iter = n_blocks // ATT_UNROLL + 2
    mask_tiles = _attn_mask_tiles()
    qk_spec = pl.BlockSpec((S, ATT_HEADS * HEAD_SLOT), lambda b, g, *_: (b, g))
    attn = pl.pallas_call(
        functools.partial(_attn_kernel, n_iter=n_iter),
        grid_spec=pltpu.PrefetchScalarGridSpec(
            num_scalar_prefetch=len(tables),
            grid=(B, H // ATT_HEADS),
            in_specs=[qk_spec, qk_spec,
                      pl.BlockSpec((None, ATT_HEADS * V_HEAD_DIM, S), lambda b, g, *_: (b, g, 0)),
                      pl.BlockSpec(mask_tiles.shape, lambda b, g, *_: (0, 0, 0))],
            out_specs=pl.BlockSpec((None, ATT_HEADS * V_HEAD_DIM, S), lambda b, g, *_: (b, g, 0)),
            scratch_shapes=[pltpu.VMEM((ATT_HEADS, 2 * ATT_UNROLL, TK, TQ), _f32),
                            pltpu.VMEM((ATT_HEADS, 2 * ATT_UNROLL, TK, TQ), _bf16)]),
        out_shape=jax.ShapeDtypeStruct((B, H * V_HEAD_DIM, S), _bf16),
        compiler_params=pltpu.CompilerParams(dimension_semantics=("arbitrary", "arbitrary"),
                                             vmem_limit_bytes=VMEM_LIMIT_ATTN),
        name="attn",
    )(*[jnp.asarray(t) for t in tables], q, k, vt, jnp.asarray(mask_tiles))

    out = pl.pallas_call(
        _post_kernel,
        grid=(T // TM_POST,),
        in_specs=[pl.BlockSpec((None, H * V_HEAD_DIM, TM_POST),
                               lambda i: (i // (S // TM_POST), 0, i % (S // TM_POST))),
                  _rows(TM_POST, D), _rows(TM_POST, D), _rows(TM_POST, D), _resident((H * V_HEAD_DIM, D)), _resident((D, D)),
                  _resident((1, D)), _resident((1, D)), _resident((D, D_FF)),
                  _resident((D_FF, D)), _resident((1, D))],
        out_specs=_rows(TM_POST, D),
        out_shape=jax.ShapeDtypeStruct((T, D), _f32),
        compiler_params=pltpu.CompilerParams(dimension_semantics=("arbitrary",),
                                             vmem_limit_bytes=VMEM_LIMIT),
        name="post",
    )(attn, sg, gc, x2, w_o_attn[0].astype(_bf16), w_out[0].astype(_bf16),
      row(norm_mix_post[0]), row(norm_mlp_pre[0]), w_ff1[0].astype(_bf16),
      w_ff2[0].astype(_bf16), row(norm_mlp_post[0]))
    return out.reshape(B, S, D)
```

```python
import functools
import math

import jax
import jax.numpy as jnp
import numpy as np
from jax import lax
from jax.experimental import pallas as pl
from jax.experimental.pallas import tpu as pltpu

D_MODEL = 1024
N_HEADS = 8
QK_NOPE_DIM = 128
QK_ROPE_DIM = 64
V_HEAD_DIM = 128
Q_LORA_RANK = 384
KV_LORA_RANK = 256
ROPE_THETA = 10000.0
CONV_CHANNELS = 1024
CONV_WIDTH = 31
D_FF = 4096
EPS = 1e-6

HALF_ROPE = QK_ROPE_DIM // 2
ROPE_PACK = 4
ROPE_LANES = ROPE_PACK * HALF_ROPE
HEAD_SLOT = QK_NOPE_DIM + ROPE_LANES
SMALL_COLS = Q_LORA_RANK + KV_LORA_RANK + ROPE_LANES
BIG_ROW0 = Q_LORA_RANK + KV_LORA_RANK + QK_ROPE_DIM
HALO = 32
SUBLANES = 8
LANES = 128
CONV_ROWS = 64
CONV_STRIDE = 2

Q_SCALE = (QK_NOPE_DIM + QK_ROPE_DIM) ** -0.5 * math.log2(math.e)
MASK_VALUE = -1e30

TM_IN = 512
TM_POST = 512
TQ = 256
TK = 512
ATT_HEADS = 2
ATT_UNROLL = 2
SUM_ROWS = 16
FF_CHUNK = 1024
POST_SPLIT = 2

V7X_VMEM_BYTES = 64 * 1024 * 1024
MIB = 1024 * 1024
VMEM_LIMIT = 46 * MIB
VMEM_LIMIT_ATTN = 36 * MIB
assert VMEM_LIMIT < V7X_VMEM_BYTES

_bf16 = jnp.bfloat16
_f32 = jnp.float32
_NT = (((1,), (1,)), ((), ()))
_TN = (((0,), (0,)), ((), ()))


def _dot(a, b):
    return jnp.dot(a, b, preferred_element_type=_f32)


def _dot_nt(a, b):
    return lax.dot_general(a, b, _NT, preferred_element_type=_f32)


def _rms(x, g):
    return x * lax.rsqrt(jnp.mean(x * x, axis=-1, keepdims=True) + EPS) * g


def _inproj_kernel(x_ref, pos_ref, invf_ref, g_pre_ref, w_small_ref, w_big_ref, qn_g_ref, w_uq_ref,
                   kvn_g_ref, w_uk_ref, w_uvt_ref, cw_ref, cb_ref, lng_ref, lnb_ref,
                   w_pw2_ref, b_pw2_ref,
                   q_ref, kn_ref, kr_ref, vt_ref, sg_ref, gc_ref,
                   ubuf, cacc, hbuf, *, tiles_per_seq):
    tm = x_ref.shape[0]
    C = CONV_CHANNELS

    n_slabs = C // LANES

    @pl.when(pl.program_id(0) % tiles_per_seq == 0)
    def _():
        ubuf[:, 0:HALO, :] = jnp.zeros((n_slabs, HALO, LANES), _f32)

    hbuf[...] = _rms(x_ref[...], g_pre_ref[...]).astype(_bf16)
    gate = jax.nn.sigmoid(_dot_nt(hbuf[...], w_big_ref[BIG_ROW0 + C:BIG_ROW0 + 2 * C, :]))
    u = _dot_nt(hbuf[...], w_big_ref[BIG_ROW0:BIG_ROW0 + C, :]) * gate
    for c in range(n_slabs):
        ubuf[c, HALO:HALO + tm, :] = u[:, c * LANES:(c + 1) * LANES]

    shift = HALO - (CONV_WIDTH - 1)
    group = CONV_STRIDE * SUBLANES

    def conv_slab(c, carry):
        for r0 in range(0, tm, CONV_ROWS):
            starts = [r0 + g * group + p for g in range(CONV_ROWS // group)
                      for p in range(CONV_STRIDE)]
            accs = [jnp.broadcast_to(cb_ref[c], (SUBLANES, LANES)) for _ in starts]
            for t in range(CONV_WIDTH):
                w = cw_ref[c, pl.ds(t, 1), :]
                for j, start in enumerate(starts):
                    rows = pl.ds(start + shift + t, SUBLANES, stride=CONV_STRIDE)
                    accs[j] = accs[j] + ubuf[c, rows, :] * w
            for acc, start in zip(accs, starts):
                cacc[c, pl.ds(start, SUBLANES, stride=CONV_STRIDE), :] = acc
        return carry

    lax.fori_loop(0, n_slabs, conv_slab, 0)
    ubuf[:, 0:HALO, :] = ubuf[:, tm:tm + HALO, :]

    conv = jnp.concatenate([cacc[c] for c in range(n_slabs)], axis=1)
    xc = conv - jnp.mean(conv, axis=-1, keepdims=True)
    var = jnp.mean(xc * xc, axis=-1, keepdims=True)
    y = xc * lax.rsqrt(var + EPS) * lng_ref[...] + lnb_ref[...]
    swish = (y * jax.nn.sigmoid(y)).astype(_bf16)

    h = hbuf[...]
    zs = _dot_nt(h, w_small_ref[...])
    cq = zs[:, 0:Q_LORA_RANK]
    ckv = zs[:, Q_LORA_RANK:Q_LORA_RANK + KV_LORA_RANK]
    kr4 = zs[:, Q_LORA_RANK + KV_LORA_RANK:SMALL_COLS]
    sg_ref[...] = jax.nn.sigmoid(
        _dot_nt(h, w_big_ref[BIG_ROW0 + 2 * C:BIG_ROW0 + 3 * C, :])).astype(_bf16)

    pos = pos_ref[...].astype(_f32)
    lane = lax.broadcasted_iota(jnp.int32, (tm // ROPE_PACK, ROPE_LANES), 1)
    in_group = [lane < (g + 1) * HALF_ROPE for g in range(ROPE_PACK - 1)]

    def by_group(parts):
        out = parts[-1]
        for g in reversed(range(ROPE_PACK - 1)):
            out = jnp.where(in_group[g], parts[g], out)
        return out

    ang = by_group([pos[:, g:g + 1] for g in range(ROPE_PACK)]) * invf_ref[...]
    cos, sin = jnp.cos(ang), jnp.sin(ang)
    cos_rot = [cos] + [pltpu.roll(cos, g * HALF_ROPE, 1) for g in range(1, ROPE_PACK)]
    sin_rot = [sin] + [pltpu.roll(sin, g * HALF_ROPE, 1) for g in range(1, ROPE_PACK)]
    rope4 = jnp.concatenate(
        [by_group([cos_rot[(0 - g) % ROPE_PACK], cos_rot[(1 - g) % ROPE_PACK],
                   -sin_rot[(2 - g) % ROPE_PACK], sin_rot[(3 - g) % ROPE_PACK]])
         for g in range(ROPE_PACK)], axis=0)

    q = _dot(_rms(cq, qn_g_ref[...]).astype(_bf16), w_uq_ref[...])
    rope4_q = rope4 * Q_SCALE
    for hd in range(N_HEADS):
        lo = hd * HEAD_SLOT
        q_ref[:, lo:lo + QK_NOPE_DIM] = (q[:, lo:lo + QK_NOPE_DIM] * Q_SCALE).astype(_bf16)
        q_ref[:, lo + QK_NOPE_DIM:lo + HEAD_SLOT] = (
            q[:, lo + QK_NOPE_DIM:lo + HEAD_SLOT] * rope4_q).astype(_bf16)

    g_conv = jax.nn.sigmoid(_dot_nt(h, w_big_ref[BIG_ROW0 + 3 * C:BIG_ROW0 + 4 * C, :]))

    kp = kr4 * rope4
    kr_ref[...] = (kp + pltpu.roll(kp, 2 * HALF_ROPE, 1)).astype(_bf16)
    ckvn = _rms(ckv, kvn_g_ref[...]).astype(_bf16)
    kn_ref[...] = _dot(ckvn, w_uk_ref[...]).astype(_bf16)
    vt_ref[...] = _dot_nt(w_uvt_ref[...], ckvn).astype(_bf16)

    y_conv = _dot(swish, w_pw2_ref[...]) + b_pw2_ref[...]
    gc_ref[...] = (g_conv * y_conv).astype(_bf16)


def _attn_stream(seq):
    qi, kj, first, last, kind = [], [], [], [], []
    for i in range(seq // TQ):
        n_blocks = (i * TQ + TQ - 1) // TK + 1
        for j in range(n_blocks):
            qi.append(i)
            kj.append(j)
            first.append(int(j == 0))
            last.append(int(j == n_blocks - 1))
            kind.append(0 if j < n_blocks - 1 else 1 + (i * TQ - j * TK) // TQ)
    pad = [0] * (2 * ATT_UNROLL)
    tables = [np.asarray(pad + t + pad, np.int32) for t in (qi, kj, first, last, kind)]
    return len(qi), tables


def _attn_mask_tiles():
    kk = np.arange(TK)[:, None]
    qq = np.arange(TQ)[None, :]
    tiles = [np.zeros((TK, TQ), np.float32)]
    for d in range(TK // TQ):
        tiles.append(np.where(kk <= qq + d * TQ, 0.0, MASK_VALUE).astype(np.float32))
    return np.stack(tiles)


def _attn_kernel(qi_ref, kj_ref, first_ref, last_ref, kind_ref,
                 q_ref, kn_ref, kr_ref, vt_ref, mask_ref, o_ref, s_buf, p_buf, *, n_iter):
    U = ATT_UNROLL
    s_buf[...] = jnp.zeros(s_buf.shape, _f32)
    p_buf[...] = jnp.zeros(p_buf.shape, _bf16)

    def stage_a(hd, e, slot):
        q0 = pl.multiple_of(qi_ref[e] * TQ, TQ)
        k0 = pl.multiple_of(kj_ref[e] * TK, TK)
        keys = jnp.concatenate(
            [kn_ref[pl.ds(k0, TK), hd * QK_NOPE_DIM:(hd + 1) * QK_NOPE_DIM],
             kr_ref[pl.ds(k0, TK), :]], axis=1)
        s = lax.dot_general(keys, q_ref[pl.ds(q0, TQ), hd * HEAD_SLOT:(hd + 1) * HEAD_SLOT],
                            _NT, preferred_element_type=_f32)
        s = s + mask_ref[kind_ref[e]]
        s_buf[hd, slot] = s
        return jnp.max(s, axis=0, keepdims=True)

    def stage_b(hd, e, slot, m, cmax):
        m_prev = jnp.where(first_ref[e] != 0, -jnp.inf, m)
        m_new = jnp.maximum(m_prev, cmax)
        p_buf[hd, slot] = jnp.exp2(s_buf[hd, slot] - m_new).astype(_bf16)
        return m_new, jnp.exp2(m_prev - m_new)

    ones_rows = jnp.ones((SUM_ROWS, TK), _bf16)

    def stage_c(hd, e, slot, acc, l, alpha):
        k0 = pl.multiple_of(kj_ref[e] * TK, TK)
        vt = vt_ref[hd * V_HEAD_DIM:(hd + 1) * V_HEAD_DIM, pl.ds(k0, TK)]
        pv = _dot(jnp.concatenate([vt, ones_rows], axis=0), p_buf[hd, slot])
        return alpha * acc + pv[:V_HEAD_DIM], alpha * l + pv[V_HEAD_DIM:V_HEAD_DIM + 1]

    def iteration(it, parity, carry):
        state = [dict(c) for c in carry]
        alpha_new = [list(st["alpha"]) for st in state]
        for u in range(U):
            for hd in range(ATT_HEADS):
                st = state[hd]
                st["m"], alpha_new[hd][u] = stage_b(hd, (it + 1) * U + u, (1 - parity) * U + u,
                                                    st["m"], st["cmax"][u])
        cmax_new = [[None] * U for _ in range(ATT_HEADS)]
        for u in range(U):
            for hd in range(ATT_HEADS):
                cmax_new[hd][u] = stage_a(hd, (it + 2) * U + u, parity * U + u)
        done = []
        for u in range(U):
            e = it * U + u
            for hd in range(ATT_HEADS):
                st = state[hd]
                st["acc"], st["l"] = stage_c(hd, e, parity * U + u, st["acc"], st["l"],
                                             st["alpha"][u])
            done.append((e, [(st["acc"], st["l"]) for st in state]))
        for hd in range(ATT_HEADS):
            state[hd]["alpha"] = tuple(alpha_new[hd])
            state[hd]["cmax"] = tuple(cmax_new[hd])
        for e, heads in done:
            @pl.when(last_ref[e] != 0)
            def _():
                q0 = pl.multiple_of(qi_ref[e] * TQ, TQ)
                for hd, (acc, l) in enumerate(heads):
                    o_ref[hd * V_HEAD_DIM:(hd + 1) * V_HEAD_DIM, pl.ds(q0, TQ)] = (
                        (acc / l).astype(_bf16))
        return tuple(state)

    def two_iterations(i, carry):
        return iteration(2 * i + 1, 1, iteration(2 * i, 0, carry))

    zeros = jnp.zeros((1, TQ), _f32)
    init = dict(m=zeros, l=zeros, acc=jnp.zeros((V_HEAD_DIM, TQ), _f32),
                alpha=(zeros,) * U, cmax=(zeros,) * U)
    lax.fori_loop(0, n_iter // 2, two_iterations, (init,) * ATT_HEADS)


def _post_kernel(attn_ref, sg_ref, gc_ref, x_ref, w_o_ref, w_out_ref, g_mix_post_ref,
                 g_mlp_pre_ref, w_ff1_ref, w_ff2_ref, g_mlp_post_ref, out_ref):
    tm = x_ref.shape[0]
    halves = [pl.ds(i * (tm // POST_SPLIT), tm // POST_SPLIT) for i in range(POST_SPLIT)]

    merged = []
    for rows in halves:
        y_attn = lax.dot_general(attn_ref[:, rows], w_o_ref[...], _TN,
                                 preferred_element_type=_f32)
        merged.append((sg_ref[rows, :].astype(_f32) * y_attn
                       + gc_ref[rows, :].astype(_f32)).astype(_bf16))
    x1, h2 = [], []
    for rows, mg in zip(halves, merged):
        x1.append(x_ref[rows, :] + _rms(_dot(mg, w_out_ref[...]), g_mix_post_ref[...]))
        h2.append(_rms(x1[-1], g_mlp_pre_ref[...]).astype(_bf16))
    f = [jnp.zeros(x.shape, _f32) for x in x1]
    for c in range(D_FF // FF_CHUNK):
        cols = slice(c * FF_CHUNK, (c + 1) * FF_CHUNK)
        for i in range(POST_SPLIT):
            hc = jnp.maximum(_dot(h2[i], w_ff1_ref[:, cols]), 0.0)
            f[i] = f[i] + _dot((hc * hc).astype(_bf16), w_ff2_ref[cols, :])
    for rows, xi, fi in zip(halves, x1, f):
        out_ref[rows, :] = xi + _rms(fi, g_mlp_post_ref[...])


def _resident(shape):
    return pl.BlockSpec(shape, lambda *_: (0,) * len(shape), pipeline_mode=pl.Buffered(1))


def _rows(tm, cols):
    return pl.BlockSpec((tm, cols), lambda i: (i, 0))


def kernel(x, positions, norm_mix_pre, w_in, q_norm, w_uq, kv_norm, w_uk, w_uv, w_o_attn,
           conv_w, conv_b, conv_ln_g, conv_ln_b, w_pw2, b_pw2, w_out, norm_mix_post,
           norm_mlp_pre, w_ff1, w_ff2, norm_mlp_post):
    B, S, D = x.shape
    T = B * S
    assert D == D_MODEL and w_in.shape[0] == 1
    assert S % TM_IN == 0 and S % TK == 0 and TK % TQ == 0 and S % TM_POST == 0
    assert N_HEADS % ATT_HEADS == 0
    C = CONV_CHANNELS
    H = N_HEADS
    tiles_per_seq = S // TM_IN

    w_in_t = jnp.swapaxes(w_in[0], 0, 1).astype(_bf16)
    o_kv, o_kr = Q_LORA_RANK + KV_LORA_RANK, Q_LORA_RANK + KV_LORA_RANK + QK_ROPE_DIM
    k1 = w_in_t[o_kv:o_kv + HALF_ROPE]
    k2 = w_in_t[o_kv + HALF_ROPE:o_kr]
    w_small = jnp.concatenate([w_in_t[:o_kv], k1, k2, k2, k1], axis=0)
    w_big = w_in_t

    wq = w_uq[0].reshape(Q_LORA_RANK, H, QK_NOPE_DIM + QK_ROPE_DIM)
    r1 = wq[:, :, QK_NOPE_DIM:QK_NOPE_DIM + HALF_ROPE]
    r2 = wq[:, :, QK_NOPE_DIM + HALF_ROPE:]
    w_uq_k = jnp.concatenate([wq[:, :, :QK_NOPE_DIM], r1, r2, r2, r1], axis=2)
    w_uq_k = w_uq_k.reshape(Q_LORA_RANK, H * HEAD_SLOT).astype(_bf16)
    w_uk_k = w_uk[0].reshape(KV_LORA_RANK, H * QK_NOPE_DIM).astype(_bf16)
    w_uvt_k = w_uv[0].reshape(KV_LORA_RANK, H * V_HEAD_DIM).T.astype(_bf16)
    cw = jnp.pad(conv_w[0].reshape(CONV_WIDTH, C), ((0, 1), (0, 0)))
    cw = cw.reshape(CONV_WIDTH + 1, C // LANES, LANES).transpose(1, 0, 2)
    cb = conv_b[0].reshape(C // LANES, 1, LANES)

    inv_freq = ROPE_THETA ** (-jnp.arange(0, QK_ROPE_DIM, 2, dtype=_f32) / QK_ROPE_DIM)
    invf4 = jnp.tile(inv_freq, 4).reshape(1, ROPE_LANES)

    row = lambda a: a.reshape(1, -1)
    x2 = x.reshape(T, D)
    pos2 = positions.reshape(T // TM_IN, ROPE_PACK, TM_IN // ROPE_PACK).transpose(0, 2, 1)

    q, kn, kr, vt, sg, gc = pl.pallas_call(
        functools.partial(_inproj_kernel, tiles_per_seq=tiles_per_seq),
        grid=(T // TM_IN,),
        in_specs=[
            _rows(TM_IN, D),
            pl.BlockSpec((None, TM_IN // ROPE_PACK, ROPE_PACK), lambda i: (i, 0, 0)),
            _resident((1, ROPE_LANES)), _resident((1, D)),
            _resident(w_small.shape), _resident(w_big.shape),
            _resident((1, Q_LORA_RANK)), _resident(w_uq_k.shape),
            _resident((1, KV_LORA_RANK)), _resident(w_uk_k.shape), _resident(w_uvt_k.shape),
            _resident(cw.shape), _resident(cb.shape), _resident((1, C)), _resident((1, C)),
            _resident((C, D)), _resident((1, D)),
        ],
        out_specs=[_rows(TM_IN, H * HEAD_SLOT), _rows(TM_IN, H * QK_NOPE_DIM),
                   _rows(TM_IN, ROPE_LANES),
                   pl.BlockSpec((None, H * V_HEAD_DIM, TM_IN),
                                lambda i: (i // tiles_per_seq, 0, i % tiles_per_seq)),
                   _rows(TM_IN, D), _rows(TM_IN, D)],
        out_shape=[jax.ShapeDtypeStruct((T, H * HEAD_SLOT), _bf16),
                   jax.ShapeDtypeStruct((T, H * QK_NOPE_DIM), _bf16),
                   jax.ShapeDtypeStruct((T, ROPE_LANES), _bf16),
                   jax.ShapeDtypeStruct((B, H * V_HEAD_DIM, S), _bf16),
                   jax.ShapeDtypeStruct((T, D), _bf16),
                   jax.ShapeDtypeStruct((T, D), _bf16)],
        scratch_shapes=[pltpu.VMEM((C // LANES, HALO + TM_IN, LANES), _f32),
                        pltpu.VMEM((C // LANES, TM_IN, LANES), _f32),
                        pltpu.VMEM((TM_IN, D), _bf16)],
        compiler_params=pltpu.CompilerParams(dimension_semantics=("arbitrary",),
                                             vmem_limit_bytes=VMEM_LIMIT),
        name="inproj",
    )(x2, pos2, invf4, row(norm_mix_pre[0]), w_small, w_big, row(q_norm[0]), w_uq_k, row(kv_norm[0]),
      w_uk_k, w_uvt_k, cw, cb, row(conv_ln_g[0]), row(conv_ln_b[0]),
      w_pw2[0].astype(_bf16), row(b_pw2[0]))

    n_blocks, tables = _attn_stream(S)
    assert n_blocks % (2 * ATT_UNROLL) == 0
    n_iter = n_blocks // ATT_UNROLL + 2
    mask_tiles = _attn_mask_tiles()
    qk_spec = pl.BlockSpec((S, ATT_HEADS * HEAD_SLOT), lambda b, g, *_: (b, g))
    attn = pl.pallas_call(
        functools.partial(_attn_kernel, n_iter=n_iter),
        grid_spec=pltpu.PrefetchScalarGridSpec(
            num_scalar_prefetch=len(tables),
            grid=(B, H // ATT_HEADS),
            in_specs=[qk_spec,
                      pl.BlockSpec((S, ATT_HEADS * QK_NOPE_DIM), lambda b, g, *_: (b, g)),
                      pl.BlockSpec((S, ROPE_LANES), lambda b, g, *_: (b, 0)),
                      pl.BlockSpec((None, ATT_HEADS * V_HEAD_DIM, S), lambda b, g, *_: (b, g, 0)),
                      pl.BlockSpec(mask_tiles.shape, lambda b, g, *_: (0, 0, 0))],
            out_specs=pl.BlockSpec((None, ATT_HEADS * V_HEAD_DIM, S), lambda b, g, *_: (b, g, 0)),
            scratch_shapes=[pltpu.VMEM((ATT_HEADS, 2 * ATT_UNROLL, TK, TQ), _f32),
                            pltpu.VMEM((ATT_HEADS, 2 * ATT_UNROLL, TK, TQ), _bf16)]),
        out_shape=jax.ShapeDtypeStruct((B, H * V_HEAD_DIM, S), _bf16),
        compiler_params=pltpu.CompilerParams(dimension_semantics=("arbitrary", "arbitrary"),
                                             vmem_limit_bytes=VMEM_LIMIT_ATTN),
        name="attn",
    )(*[jnp.asarray(t) for t in tables], q, kn, kr, vt, jnp.asarray(mask_tiles))

    out = pl.pallas_call(
        _post_kernel,
        grid=(T // TM_POST,),
        in_specs=[pl.BlockSpec((None, H * V_HEAD_DIM, TM_POST),
                               lambda i: (i // (S // TM_POST), 0, i % (S // TM_POST))),
                  _rows(TM_POST, D), _rows(TM_POST, D), _rows(TM_POST, D), _resident((H * V_HEAD_DIM, D)), _resident((D, D)),
                  _resident((1, D)), _resident((1, D)), _resident((D, D_FF)),
                  _resident((D_FF, D)), _resident((1, D))],
        out_specs=_rows(TM_POST, D),
        out_shape=jax.ShapeDtypeStruct((T, D), _f32),
        compiler_params=pltpu.CompilerParams(dimension_semantics=("arbitrary",),
                                             vmem_limit_bytes=VMEM_LIMIT),
        name="post",
    )(attn, sg, gc, x2, w_o_attn[0].astype(_bf16), w_out[0].astype(_bf16),
      row(norm_mix_post[0]), row(norm_mlp_pre[0]), w_ff1[0].astype(_bf16),
      w_ff2[0].astype(_bf16), row(norm_mlp_post[0]))
    return out.reshape(B, S, D)
```

```python
import functools
import math

import jax
import jax.numpy as jnp
import numpy as np
from jax import lax
from jax.experimental import pallas as pl
from jax.experimental.pallas import tpu as pltpu

D_MODEL = 1024
N_HEADS = 8
QK_NOPE_DIM = 128
QK_ROPE_DIM = 64
V_HEAD_DIM = 128
Q_LORA_RANK = 384
KV_LORA_RANK = 256
ROPE_THETA = 10000.0
CONV_CHANNELS = 1024
CONV_WIDTH = 31
D_FF = 4096
EPS = 1e-6

HALF_ROPE = QK_ROPE_DIM // 2
ROPE_PACK = 4
ROPE_LANES = ROPE_PACK * HALF_ROPE
HEAD_SLOT = QK_NOPE_DIM + ROPE_LANES
SMALL_COLS = Q_LORA_RANK + KV_LORA_RANK + ROPE_LANES
BIG_ROW0 = Q_LORA_RANK + KV_LORA_RANK + QK_ROPE_DIM
HALO = 32
SUBLANES = 8
LANES = 128
CONV_ROWS = 64
CONV_STRIDE = 2

Q_SCALE = (QK_NOPE_DIM + QK_ROPE_DIM) ** -0.5 * math.log2(math.e)
MASK_VALUE = -1e30

TM_IN = 512
TM_POST = 512
TQ = 256
TK = 512
ATT_HEADS = 2
ATT_UNROLL = 2
SUM_ROWS = 16
FF_CHUNK = 1024
POST_SPLIT = 2

V7X_VMEM_BYTES = 64 * 1024 * 1024
MIB = 1024 * 1024
VMEM_LIMIT = 46 * MIB
VMEM_LIMIT_ATTN = 36 * MIB
assert VMEM_LIMIT < V7X_VMEM_BYTES

_bf16 = jnp.bfloat16
_f32 = jnp.float32
_NT = (((1,), (1,)), ((), ()))
_TN = (((0,), (0,)), ((), ()))


def _dot(a, b):
    return jnp.dot(a, b, preferred_element_type=_f32)


def _dot_nt(a, b):
    return lax.dot_general(a, b, _NT, preferred_element_type=_f32)


def _rms(x, g):
    return x * lax.rsqrt(jnp.mean(x * x, axis=-1, keepdims=True) + EPS) * g


def _inproj_kernel(x_ref, pos_ref, invf_ref, g_pre_ref, w_small_ref, w_big_ref, qn_g_ref, w_uq_ref,
                   kvn_g_ref, w_uk_ref, w_uvt_ref, cw_ref, cb_ref, lng_ref, lnb_ref,
                   w_pw2_ref, b_pw2_ref,
                   q_ref, k_ref, vt_ref, sg_ref, gc_ref,
                   ubuf, cacc, hbuf, *, tiles_per_seq):
    tm = x_ref.shape[0]
    C = CONV_CHANNELS

    n_slabs = C // LANES

    @pl.when(pl.program_id(0) % tiles_per_seq == 0)
    def _():
        ubuf[:, 0:HALO, :] = jnp.zeros((n_slabs, HALO, LANES), _f32)

    hbuf[...] = _rms(x_ref[...], g_pre_ref[...]).astype(_bf16)
    gate = jax.nn.sigmoid(_dot_nt(hbuf[...], w_big_ref[BIG_ROW0 + C:BIG_ROW0 + 2 * C, :]))
    u = _dot_nt(hbuf[...], w_big_ref[BIG_ROW0:BIG_ROW0 + C, :]) * gate
    for c in range(n_slabs):
        ubuf[c, HALO:HALO + tm, :] = u[:, c * LANES:(c + 1) * LANES]

    shift = HALO - (CONV_WIDTH - 1)
    group = CONV_STRIDE * SUBLANES

    def conv_slab(c, carry):
        for r0 in range(0, tm, CONV_ROWS):
            starts = [r0 + g * group + p for g in range(CONV_ROWS // group)
                      for p in range(CONV_STRIDE)]
            accs = [jnp.broadcast_to(cb_ref[c], (SUBLANES, LANES)) for _ in starts]
            for t in range(CONV_WIDTH):
                w = cw_ref[c, pl.ds(t, 1), :]
                for j, start in enumerate(starts):
                    rows = pl.ds(start + shift + t, SUBLANES, stride=CONV_STRIDE)
                    accs[j] = accs[j] + ubuf[c, rows, :] * w
            for acc, start in zip(accs, starts):
                cacc[c, pl.ds(start, SUBLANES, stride=CONV_STRIDE), :] = acc
        return carry

    lax.fori_loop(0, n_slabs, conv_slab, 0)
    ubuf[:, 0:HALO, :] = ubuf[:, tm:tm + HALO, :]

    conv = jnp.concatenate([cacc[c] for c in range(n_slabs)], axis=1)
    xc = conv - jnp.mean(conv, axis=-1, keepdims=True)
    var = jnp.mean(xc * xc, axis=-1, keepdims=True)
    y = xc * lax.rsqrt(var + EPS) * lng_ref[...] + lnb_ref[...]
    swish = (y * jax.nn.sigmoid(y)).astype(_bf16)

    h = hbuf[...]
    zs = _dot_nt(h, w_small_ref[...])
    cq = zs[:, 0:Q_LORA_RANK]
    ckv = zs[:, Q_LORA_RANK:Q_LORA_RANK + KV_LORA_RANK]
    kr4 = zs[:, Q_LORA_RANK + KV_LORA_RANK:SMALL_COLS]
    sg_ref[...] = jax.nn.sigmoid(
        _dot_nt(h, w_big_ref[BIG_ROW0 + 2 * C:BIG_ROW0 + 3 * C, :])).astype(_bf16)

    pos = pos_ref[...].astype(_f32)
    lane = lax.broadcasted_iota(jnp.int32, (tm // ROPE_PACK, ROPE_LANES), 1)
    in_group = [lane < (g + 1) * HALF_ROPE for g in range(ROPE_PACK - 1)]

    def by_group(parts):
        out = parts[-1]
        for g in reversed(range(ROPE_PACK - 1)):
            out = jnp.where(in_group[g], parts[g], out)
        return out

    ang = by_group([pos[:, g:g + 1] for g in range(ROPE_PACK)]) * invf_ref[...]
    cos, sin = jnp.cos(ang), jnp.sin(ang)
    cos_rot = [cos] + [pltpu.roll(cos, g * HALF_ROPE, 1) for g in range(1, ROPE_PACK)]
    sin_rot = [sin] + [pltpu.roll(sin, g * HALF_ROPE, 1) for g in range(1, ROPE_PACK)]
    rope4 = jnp.concatenate(
        [by_group([cos_rot[(0 - g) % ROPE_PACK], cos_rot[(1 - g) % ROPE_PACK],
                   -sin_rot[(2 - g) % ROPE_PACK], sin_rot[(3 - g) % ROPE_PACK]])
         for g in range(ROPE_PACK)], axis=0)

    q = _dot(_rms(cq, qn_g_ref[...]).astype(_bf16), w_uq_ref[...])
    rope4_q = rope4 * Q_SCALE
    for hd in range(N_HEADS):
        lo = hd * HEAD_SLOT
        q_ref[:, lo:lo + QK_NOPE_DIM] = (q[:, lo:lo + QK_NOPE_DIM] * Q_SCALE).astype(_bf16)
        q_ref[:, lo + QK_NOPE_DIM:lo + HEAD_SLOT] = (
            q[:, lo + QK_NOPE_DIM:lo + HEAD_SLOT] * rope4_q).astype(_bf16)

    g_conv = jax.nn.sigmoid(_dot_nt(h, w_big_ref[BIG_ROW0 + 3 * C:BIG_ROW0 + 4 * C, :]))

    kp = kr4 * rope4
    kdup = (kp + pltpu.roll(kp, 2 * HALF_ROPE, 1)).astype(_bf16)
    ckvn = _rms(ckv, kvn_g_ref[...]).astype(_bf16)
    kn = _dot(ckvn, w_uk_ref[...])
    for hd in range(N_HEADS):
        lo = hd * HEAD_SLOT
        k_ref[:, lo:lo + QK_NOPE_DIM] = kn[:, hd * QK_NOPE_DIM:(hd + 1) * QK_NOPE_DIM].astype(_bf16)
        k_ref[:, lo + QK_NOPE_DIM:lo + HEAD_SLOT] = kdup
    vt_ref[...] = _dot_nt(w_uvt_ref[...], ckvn).astype(_bf16)

    y_conv = _dot(swish, w_pw2_ref[...]) + b_pw2_ref[...]
    gc_ref[...] = (g_conv * y_conv).astype(_bf16)


def _attn_stream(seq):
    qi, kj, first, last, kind = [], [], [], [], []
    for i in range(seq // TQ):
        n_blocks = (i * TQ + TQ - 1) // TK + 1
        for j in range(n_blocks):
            qi.append(i)
            kj.append(j)
            first.append(int(j == 0))
            last.append(int(j == n_blocks - 1))
            kind.append(0 if j < n_blocks - 1 else 1 + (i * TQ - j * TK) // TQ)
    pad = [0] * (2 * ATT_UNROLL)
    tables = [np.asarray(pad + t + pad, np.int32) for t in (qi, kj, first, last, kind)]
    return len(qi), tables


def _attn_mask_tiles():
    kk = np.arange(TK)[:, None]
    qq = np.arange(TQ)[None, :]
    tiles = [np.zeros((TK, TQ), np.float32)]
    for d in range(TK // TQ):
        tiles.append(np.where(kk <= qq + d * TQ, 0.0, MASK_VALUE).astype(np.float32))
    return np.stack(tiles)


def _attn_kernel(qi_ref, kj_ref, first_ref, last_ref, kind_ref,
                 q_ref, k_ref, vt_ref, mask_ref, o_ref, s_buf, p_buf, *, n_iter):
    U = ATT_UNROLL
    s_buf[:, U:2 * U] = jnp.zeros((ATT_HEADS, U, TK, TQ), _f32)
    p_buf[:, 0:U] = jnp.zeros((ATT_HEADS, U, TK, TQ), _bf16)

    def head_cols(ref, hd, width, rows):
        return ref[rows, hd * width:(hd + 1) * width]

    def stage_a(hd, e, slot):
        q0 = pl.multiple_of(qi_ref[e] * TQ, TQ)
        k0 = pl.multiple_of(kj_ref[e] * TK, TK)
        s = lax.dot_general(head_cols(k_ref, hd, HEAD_SLOT, pl.ds(k0, TK)),
                            head_cols(q_ref, hd, HEAD_SLOT, pl.ds(q0, TQ)), _NT,
                            preferred_element_type=_f32)
        s = s + mask_ref[kind_ref[e]]
        s_buf[hd, slot] = s
        return jnp.max(s, axis=0, keepdims=True)

    def stage_b(hd, e, slot, m, cmax):
        m_prev = jnp.where(first_ref[e] != 0, -jnp.inf, m)
        m_new = jnp.maximum(m_prev, cmax)
        p_buf[hd, slot] = jnp.exp2(s_buf[hd, slot] - m_new).astype(_bf16)
        return m_new, jnp.exp2(m_prev - m_new)

    ones_rows = jnp.ones((SUM_ROWS, TK), _bf16)

    def stage_c(hd, e, slot, acc, l, alpha):
        k0 = pl.multiple_of(kj_ref[e] * TK, TK)
        vt = vt_ref[hd * V_HEAD_DIM:(hd + 1) * V_HEAD_DIM, pl.ds(k0, TK)]
        pv = _dot(jnp.concatenate([vt, ones_rows], axis=0), p_buf[hd, slot])
        return alpha * acc + pv[:V_HEAD_DIM], alpha * l + pv[V_HEAD_DIM:V_HEAD_DIM + 1]

    def iteration(it, parity, carry):
        state = [dict(c) for c in carry]
        alpha_new = [list(st["alpha"]) for st in state]
        for u in range(U):
            for hd in range(ATT_HEADS):
                st = state[hd]
                st["m"], alpha_new[hd][u] = stage_b(hd, (it + 1) * U + u, (1 - parity) * U + u,
                                                    st["m"], st["cmax"][u])
        cmax_new = [[None] * U for _ in range(ATT_HEADS)]
        for u in range(U):
            for hd in range(ATT_HEADS):
                cmax_new[hd][u] = stage_a(hd, (it + 2) * U + u, parity * U + u)
        done = []
        for u in range(U):
            e = it * U + u
            for hd in range(ATT_HEADS):
                st = state[hd]
                st["acc"], st["l"] = stage_c(hd, e, parity * U + u, st["acc"], st["l"],
                                             st["alpha"][u])
            done.append((e, [(st["acc"], st["l"]) for st in state]))
        for hd in range(ATT_HEADS):
            state[hd]["alpha"] = tuple(alpha_new[hd])
            state[hd]["cmax"] = tuple(cmax_new[hd])
        for e, heads in done:
            @pl.when(last_ref[e] != 0)
            def _():
                q0 = pl.multiple_of(qi_ref[e] * TQ, TQ)
                for hd, (acc, l) in enumerate(heads):
                    o_ref[hd * V_HEAD_DIM:(hd + 1) * V_HEAD_DIM, pl.ds(q0, TQ)] = (
                        (acc / l).astype(_bf16))
        return tuple(state)

    def two_iterations(i, carry):
        return iteration(2 * i + 1, 1, iteration(2 * i, 0, carry))

    zeros = jnp.zeros((1, TQ), _f32)
    init = dict(m=zeros, l=zeros, acc=jnp.zeros((V_HEAD_DIM, TQ), _f32),
                alpha=(zeros,) * U, cmax=(zeros,) * U)
    lax.fori_loop(0, n_iter // 2, two_iterations, (init,) * ATT_HEADS)


def _post_kernel(attn_ref, sg_ref, gc_ref, x_ref, w_o_ref, w_out_ref, g_mix_post_ref,
                 g_mlp_pre_ref, w_ff1_ref, w_ff2_ref, g_mlp_post_ref, out_ref):
    tm = x_ref.shape[0]
    halves = [pl.ds(i * (tm // POST_SPLIT), tm // POST_SPLIT) for i in range(POST_SPLIT)]

    merged = []
    for rows in halves:
        y_attn = lax.dot_general(attn_ref[:, rows], w_o_ref[...], _TN,
                                 preferred_element_type=_f32)
        merged.append((sg_ref[rows, :].astype(_f32) * y_attn
                       + gc_ref[rows, :].astype(_f32)).astype(_bf16))
    x1, h2 = [], []
    for rows, mg in zip(halves, merged):
        x1.append(x_ref[rows, :] + _rms(_dot(mg, w_out_ref[...]), g_mix_post_ref[...]))
        h2.append(_rms(x1[-1], g_mlp_pre_ref[...]).astype(_bf16))
    f = [jnp.zeros(x.shape, _f32) for x in x1]
    for c in range(D_FF // FF_CHUNK):
        cols = slice(c * FF_CHUNK, (c + 1) * FF_CHUNK)
        for i in range(POST_SPLIT):
            hc = jnp.maximum(_dot(h2[i], w_ff1_ref[:, cols]), 0.0)
            f[i] = f[i] + _dot((hc * hc).astype(_bf16), w_ff2_ref[cols, :])
    for rows, xi, fi in zip(halves, x1, f):
        out_ref[rows, :] = xi + _rms(fi, g_mlp_post_ref[...])


def _resident(shape):
    return pl.BlockSpec(shape, lambda *_: (0,) * len(shape), pipeline_mode=pl.Buffered(1))


def _rows(tm, cols):
    return pl.BlockSpec((tm, cols), lambda i: (i, 0))


def kernel(x, positions, norm_mix_pre, w_in, q_norm, w_uq, kv_norm, w_uk, w_uv, w_o_attn,
           conv_w, conv_b, conv_ln_g, conv_ln_b, w_pw2, b_pw2, w_out, norm_mix_post,
           norm_mlp_pre, w_ff1, w_ff2, norm_mlp_post):
    B, S, D = x.shape
    T = B * S
    assert D == D_MODEL and w_in.shape[0] == 1
    assert S % TM_IN == 0 and S % TK == 0 and TK % TQ == 0 and S % TM_POST == 0
    assert N_HEADS % ATT_HEADS == 0
    C = CONV_CHANNELS
    H = N_HEADS
    tiles_per_seq = S // TM_IN

    w_in_t = jnp.swapaxes(w_in[0], 0, 1).astype(_bf16)
    o_kv, o_kr = Q_LORA_RANK + KV_LORA_RANK, Q_LORA_RANK + KV_LORA_RANK + QK_ROPE_DIM
    k1 = w_in_t[o_kv:o_kv + HALF_ROPE]
    k2 = w_in_t[o_kv + HALF_ROPE:o_kr]
    w_small = jnp.concatenate([w_in_t[:o_kv], k1, k2, k2, k1], axis=0)
    w_big = w_in_t

    wq = w_uq[0].reshape(Q_LORA_RANK, H, QK_NOPE_DIM + QK_ROPE_DIM)
    r1 = wq[:, :, QK_NOPE_DIM:QK_NOPE_DIM + HALF_ROPE]
    r2 = wq[:, :, QK_NOPE_DIM + HALF_ROPE:]
    w_uq_k = jnp.concatenate([wq[:, :, :QK_NOPE_DIM], r1, r2, r2, r1], axis=2)
    w_uq_k = w_uq_k.reshape(Q_LORA_RANK, H * HEAD_SLOT).astype(_bf16)
    w_uk_k = w_uk[0].reshape(KV_LORA_RANK, H * QK_NOPE_DIM).astype(_bf16)
    w_uvt_k = w_uv[0].reshape(KV_LORA_RANK, H * V_HEAD_DIM).T.astype(_bf16)
    cw = jnp.pad(conv_w[0].reshape(CONV_WIDTH, C), ((0, 1), (0, 0)))
    cw = cw.reshape(CONV_WIDTH + 1, C // LANES, LANES).transpose(1, 0, 2)
    cb = conv_b[0].reshape(C // LANES, 1, LANES)

    inv_freq = ROPE_THETA ** (-jnp.arange(0, QK_ROPE_DIM, 2, dtype=_f32) / QK_ROPE_DIM)
    invf4 = jnp.tile(inv_freq, 4).reshape(1, ROPE_LANES)

    row = lambda a: a.reshape(1, -1)
    x2 = x.reshape(T, D)
    pos2 = positions.reshape(T // TM_IN, ROPE_PACK, TM_IN // ROPE_PACK).transpose(0, 2, 1)

    q, k, vt, sg, gc = pl.pallas_call(
        functools.partial(_inproj_kernel, tiles_per_seq=tiles_per_seq),
        grid=(T // TM_IN,),
        in_specs=[
            _rows(TM_IN, D),
            pl.BlockSpec((None, TM_IN // ROPE_PACK, ROPE_PACK), lambda i: (i, 0, 0)),
            _resident((1, ROPE_LANES)), _resident((1, D)),
            _resident(w_small.shape), _resident(w_big.shape),
            _resident((1, Q_LORA_RANK)), _resident(w_uq_k.shape),
            _resident((1, KV_LORA_RANK)), _resident(w_uk_k.shape), _resident(w_uvt_k.shape),
            _resident(cw.shape), _resident(cb.shape), _resident((1, C)), _resident((1, C)),
            _resident((C, D)), _resident((1, D)),
        ],
        out_specs=[_rows(TM_IN, H * HEAD_SLOT), _rows(TM_IN, H * HEAD_SLOT),
                   pl.BlockSpec((None, H * V_HEAD_DIM, TM_IN),
                                lambda i: (i // tiles_per_seq, 0, i % tiles_per_seq)),
                   _rows(TM_IN, D), _rows(TM_IN, D)],
        out_shape=[jax.ShapeDtypeStruct((T, H * HEAD_SLOT), _bf16),
                   jax.ShapeDtypeStruct((T, H * HEAD_SLOT), _bf16),
                   jax.ShapeDtypeStruct((B, H * V_HEAD_DIM, S), _bf16),
                   jax.ShapeDtypeStruct((T, D), _bf16),
                   jax.ShapeDtypeStruct((T, D), _bf16)],
        scratch_shapes=[pltpu.VMEM((C // LANES, HALO + TM_IN, LANES), _f32),
                        pltpu.VMEM((C // LANES, TM_IN, LANES), _f32),
                        pltpu.VMEM((TM_IN, D), _bf16)],
        compiler_params=pltpu.CompilerParams(dimension_semantics=("arbitrary",),
                                             vmem_limit_bytes=VMEM_LIMIT),
        name="inproj",
    )(x2, pos2, invf4, row(norm_mix_pre[0]), w_small, w_big, row(q_norm[0]), w_uq_k, row(kv_norm[0]),
      w_uk_k, w_uvt_k, cw, cb, row(conv_ln_g[0]), row(conv_ln_b[0]),
      w_pw2[0].astype(_bf16), row(b_pw2[0]))

    n_blocks, tables = _attn_stream(S)
    assert n_blocks % (2 * ATT_UNROLL) == 0
    n_iter = n_blocks // ATT_UNROLL + 2
    mask_tiles = _attn_mask_tiles()
    qk_spec = pl.BlockSpec((S, ATT_HEADS * HEAD_SLOT), lambda b, g, *_: (b, g))
    attn = pl.pallas_call(
        functools.partial(_attn_kernel, n_iter=n_iter),
        grid_spec=pltpu.PrefetchScalarGridSpec(
            num_scalar_prefetch=len(tables),
            grid=(B, H // ATT_HEADS),
            in_specs=[qk_spec, qk_spec,
                      pl.BlockSpec((None, ATT_HEADS * V_HEAD_DIM, S), lambda b, g, *_: (b, g, 0)),
                      pl.BlockSpec(mask_tiles.shape, lambda b, g, *_: (0, 0, 0))],
            out_specs=pl.BlockSpec((None, ATT_HEADS * V_HEAD_DIM, S), lambda b, g, *_: (b, g, 0)),
            scratch_shapes=[pltpu.VMEM((ATT_HEADS, 2 * ATT_UNROLL, TK, TQ), _f32),
                            pltpu.VMEM((ATT_HEADS, 2 * ATT_UNROLL, TK, TQ), _bf16)]),
        out_shape=jax.ShapeDtypeStruct((B, H * V_HEAD_DIM, S), _bf16),
        compiler_params=pltpu.CompilerParams(dimension_semantics=("arbitrary", "arbitrary"),
                                             vmem_limit_bytes=VMEM_LIMIT_ATTN),
        name="attn",
    )(*[jnp.asarray(t) for t in tables], q, k, vt, jnp.asarray(mask_tiles))

    out = pl.pallas_call(
        _post_kernel,
        grid=(T // TM_POST,),
        in_specs=[pl.BlockSpec((None, H * V_HEAD_DIM, TM_POST),
                               lambda i: (i // (S // TM_POST), 0, i % (S // TM_POST))),
                  _rows(TM_POST, D), _rows(TM_POST, D), _rows(TM_POST, D), _resident((H * V_HEAD_DIM, D)), _resident((D, D)),
                  _resident((1, D)), _resident((1, D)), _resident((D, D_FF)),
                  _resident((D_FF, D)), _resident((1, D))],
        out_specs=_rows(TM_POST, D),
        out_shape=jax.ShapeDtypeStruct((T, D), _f32),
        compiler_params=pltpu.CompilerParams(dimension_semantics=("arbitrary",),
                                             vmem_limit_bytes=VMEM_LIMIT),
        name="post",
    )(attn, sg, gc, x2, w_o_attn[0].astype(_bf16), w_out[0].astype(_bf16),
      row(norm_mix_post[0]), row(norm_mlp_pre[0]), w_ff1[0].astype(_bf16),
      w_ff2[0].astype(_bf16), row(norm_mlp_post[0]))
    return out.reshape(B, S, D)
```
